```python
import math
import jax, jax.numpy as jnp
from jax import lax
import numpy as np

D_MODEL = 4096
BATCH = 8
SEQ = 2048
DEPTH = 2

CHUNK = 64
MIX_WIDTH = D_MODEL
DN_WIDTH = MIX_WIDTH // 2
DN_HEAD_DIM = 128
DN_HEADS = DN_WIDTH // DN_HEAD_DIM
DN_CONV = 4
DN_NORM_EPS = 1e-6
RW_WIDTH = MIX_WIDTH - DN_WIDTH
RW_HEAD_DIM = 64
RW_HEADS = RW_WIDTH // RW_HEAD_DIM
RW_DECAY_RANK = 96
RW_AAA_RANK = 96
RW_MV_RANK = 64
RW_GATE_RANK = 256
RW_GN_EPS = 64e-5
N_EXPERTS = 32
TOP_K = 4
D_EXPERT = 768
SWIGLU_ALPHA = 1.702
SWIGLU_LIMIT = 7.0
LN_EPS = 1e-5
DEEPNORM_ALPHA = (2 * DEPTH) ** 0.25
DEEPNORM_BETA = (8 * DEPTH) ** -0.25
DN_SPLITS = (DN_WIDTH, DN_WIDTH, DN_WIDTH, DN_WIDTH, DN_HEADS, DN_HEADS)
RW_SPLITS = (RW_WIDTH, RW_WIDTH, RW_WIDTH, RW_DECAY_RANK, RW_AAA_RANK, RW_GATE_RANK)
N_DN_IN = sum(DN_SPLITS)
N_RW_IN = sum(RW_SPLITS)
N_IN = N_DN_IN + N_RW_IN

kernel_name = 'hybrid_gdn_rwkv7_moe_deepnorm'


def split_last(t, sizes):
    out, start = [], 0
    for s in sizes:
        out.append(t[..., start:start + s])
        start += s
    return out


def layer_norm(x, g, b):
    xf = x.astype(jnp.float32)
    mu = jnp.mean(xf, axis=-1, keepdims=True)
    var = jnp.mean(jnp.square(xf - mu), axis=-1, keepdims=True)
    return ((xf - mu) * lax.rsqrt(var + LN_EPS) * g + b).astype(x.dtype)


def l2_normalize(t, eps):
    return t * lax.rsqrt(jnp.sum(jnp.square(t), axis=-1, keepdims=True) + eps)


def token_shift(t):
    return jnp.pad(t, ((0, 0), (1, 0), (0, 0)))[:, :-1]


def causal_depthwise_conv(x, w):
    k_len, ch = w.shape
    return lax.conv_general_dilated(x, w[:, None, :], window_strides=(1,), padding=((k_len - 1, 0),),
                                    dimension_numbers=('NWC', 'WIO', 'NWC'), feature_group_count=ch)


def chunk_gated_delta_rule(q, k, v, g, beta):
    bsz, seq, nh, dk = q.shape
    dv = v.shape[-1]
    nc = seq // CHUNK

    def blocks(t):
        return t.reshape(bsz, nc, CHUNK, nh, -1).transpose(0, 3, 1, 2, 4)

    q, k, v = blocks(q), blocks(k), blocks(v)
    g = blocks(g[..., None])[..., 0]
    beta = blocks(beta[..., None])[..., 0]
    g_cum = jnp.cumsum(g, axis=-1)
    idx = jnp.arange(CHUNK)
    causal = idx[:, None] >= idx[None, :]
    strict = idx[:, None] > idx[None, :]
    decay = jnp.exp(jnp.where(causal, g_cum[..., :, None] - g_cum[..., None, :], -jnp.inf))
    k_beta = k * beta[..., None]
    lower = jnp.where(strict, jnp.einsum('bhnid,bhnjd->bhnij', k_beta, k) * decay, 0.0)
    eye = jnp.eye(CHUNK, dtype=q.dtype)
    t_inv = lax.linalg.triangular_solve(eye + lower, jnp.broadcast_to(eye, lower.shape),
                                        left_side=True, lower=True, unit_diagonal=True)
    u = jnp.einsum('bhnij,bhnjd->bhnid', t_inv, v * beta[..., None])
    w = jnp.einsum('bhnij,bhnjd->bhnid', t_inv, k_beta * jnp.exp(g_cum)[..., None])
    qk = jnp.einsum('bhnid,bhnjd->bhnij', q, k) * decay
    q_dec = q * jnp.exp(g_cum)[..., None]
    k_dec = k * jnp.exp(g_cum[..., -1:] - g_cum)[..., None]
    chunk_decay = jnp.exp(g_cum[..., -1])

    def step(state, inp):
        u_c, w_c, qk_c, qd_c, kd_c, cd_c = inp
        v_new = u_c - jnp.einsum('bhcd,bhde->bhce', w_c, state)
        out = jnp.einsum('bhcd,bhde->bhce', qd_c, state) + jnp.einsum('bhij,bhje->bhie', qk_c, v_new)
        state = state * cd_c[..., None, None] + jnp.einsum('bhcd,bhce->bhde', kd_c, v_new)
        return state, out

    xs = tuple(jnp.moveaxis(t, 2, 0) for t in (u, w, qk, q_dec, k_dec, chunk_decay))
    state0 = jnp.zeros((bsz, nh, dk, dv), q.dtype)
    _, out = lax.scan(step, state0, xs)
    return out.transpose(1, 0, 3, 2, 4).reshape(bsz, seq, nh, dv)


def gated_deltanet_group(p, conv_w, a_log, dt_bias, norm_w):
    f32 = jnp.float32
    p = p.astype(f32)
    bsz, seq = p.shape[:2]
    qkv = jax.nn.silu(causal_depthwise_conv(p[..., :3 * DN_WIDTH], conv_w.astype(f32)))
    q, k, v = split_last(qkv, (DN_WIDTH, DN_WIDTH, DN_WIDTH))
    z, b, a = split_last(p[..., 3 * DN_WIDTH:], (DN_WIDTH, DN_HEADS, DN_HEADS))

    def heads(t):
        return t.reshape(bsz, seq, DN_HEADS, DN_HEAD_DIM)

    q = l2_normalize(heads(q), 1e-6) * DN_HEAD_DIM ** -0.5
    k = l2_normalize(heads(k), 1e-6)
    beta = jax.nn.sigmoid(b)
    g = -jnp.exp(a_log.astype(f32)) * jax.nn.softplus(a + dt_bias.astype(f32))
    o = chunk_gated_delta_rule(q, k, heads(v), g, beta)
    o = o * lax.rsqrt(jnp.mean(jnp.square(o), axis=-1, keepdims=True) + DN_NORM_EPS)
    o = o * norm_w.astype(f32) * jax.nn.silu(heads(z))
    return o.reshape(bsz, seq, DN_WIDTH)


def rwkv7_scan(r, d, k, v, a_vec, b_vec):
    bsz, seq, nh, n = r.shape

    def step(state, inp):
        r_t, d_t, k_t, v_t, a_t, b_t = inp
        sa = jnp.einsum('bhvk,bhk->bhv', state, a_t)
        state = (state * d_t[:, :, None, :] + sa[..., None] * b_t[:, :, None, :]
                 + v_t[..., None] * k_t[:, :, None, :])
        return state, jnp.einsum('bhvk,bhk->bhv', state, r_t)

    xs = tuple(jnp.moveaxis(t, 1, 0) for t in (r, d, k, v, a_vec, b_vec))
    state0 = jnp.zeros((bsz, nh, n, n), r.dtype)
    _, y = lax.scan(step, state0, xs)
    return jnp.moveaxis(y, 0, 1)


def rwkv7_group(p, w0, w2, a0, a2, g2, k_k, k_a, r_k, gn_w, gn_b, v_first, v_mix):
    f32 = jnp.float32
    p = p.astype(f32)
    bsz, seq = p.shape[:2]
    if v_mix is None:
        r, k, v, w_lo, a_lo, g_lo = split_last(p, RW_SPLITS)
        v_first = v
    else:
        r, k, v, w_lo, a_lo, g_lo, v_lo = split_last(p, RW_SPLITS + (RW_MV_RANK,))
        v0, v2 = v_mix
        v = v + (v_first - v) * jax.nn.sigmoid(v0 + v_lo @ v2)
    w = -jax.nn.softplus(-(w0 + jnp.tanh(w_lo) @ w2)) - 0.5
    decay = jnp.exp(-jnp.exp(w))
    a = jax.nn.sigmoid(a0 + a_lo @ a2)
    g = jax.nn.sigmoid(g_lo) @ g2

    def heads(t):
        return t.reshape(bsz, seq, RW_HEADS, RW_HEAD_DIM)

    kk = l2_normalize(heads(k * k_k), 1e-12)
    k = k * (1.0 + (a - 1.0) * k_a)
    rh, kh, vh = heads(r), heads(k), heads(v)
    y = rwkv7_scan(rh, heads(decay), kh, vh, -kk, kk * heads(a))
    mu = jnp.mean(y, axis=-1, keepdims=True)
    var = jnp.mean(jnp.square(y - mu), axis=-1, keepdims=True)
    y = ((y - mu) * lax.rsqrt(var + RW_GN_EPS)).reshape(bsz, seq, RW_WIDTH) * gn_w + gn_b
    bonus = jnp.sum(rh * kh * r_k, axis=-1, keepdims=True) * vh
    y = (y + bonus.reshape(bsz, seq, RW_WIDTH)) * g
    return y, v_first


def moe_ffn(x, router_w, router_b, w_gate, b_gate, w_up, b_up, w_down, b_down):
    f32 = jnp.float32
    xt = x.reshape(-1, D_MODEL)
    logits = (xt @ router_w + router_b).astype(f32)
    top_v, top_i = lax.top_k(logits, TOP_K)
    top_w = jax.nn.softmax(top_v, axis=-1)
    combine = jnp.sum(jax.nn.one_hot(top_i, N_EXPERTS, dtype=f32) * top_w[..., None], axis=1)
    y = jnp.zeros(xt.shape, f32)
    for e in range(N_EXPERTS):
        gate = jnp.minimum(xt @ w_gate[e] + b_gate[e], SWIGLU_LIMIT)
        up = jnp.clip(xt @ w_up[e] + b_up[e], -SWIGLU_LIMIT, SWIGLU_LIMIT)
        h = (up + 1.0) * gate * jax.nn.sigmoid(gate * SWIGLU_ALPHA)
        y = y + combine[:, e:e + 1] * (h @ w_down[e] + b_down[e])
    return y.astype(x.dtype).reshape(x.shape)


def setup_inputs(seed: int = 0) -> dict:
    key = jax.random.key(seed)
    ks = iter(jax.random.split(key, 64))
    f32 = jnp.float32
    nl, nl1 = DEPTH, DEPTH - 1

    def nrm(shape, scale):
        return scale * jax.random.normal(next(ks), shape, f32)

    def uni(shape, lo, hi):
        return jax.random.uniform(next(ks), shape, f32, lo, hi)

    dt = jnp.exp(uni((nl, DN_HEADS), math.log(1e-3), math.log(1e-1)))
    return {
        'x': nrm((BATCH, SEQ, D_MODEL), 1.0),
        'mix_w_in': nrm((nl, D_MODEL, N_IN), D_MODEL ** -0.5),
        'mix_w_in_vres': nrm((nl1, D_MODEL, RW_MV_RANK), D_MODEL ** -0.5),
        'dn_conv_w': nrm((nl, DN_CONV, 3 * DN_WIDTH), DN_CONV ** -0.5),
        'dn_a_log': jnp.log(uni((nl, DN_HEADS), 1.0, 16.0)),
        'dn_dt_bias': dt + jnp.log(-jnp.expm1(-dt)),
        'dn_norm_w': 1.0 + nrm((nl, DN_HEAD_DIM), 0.02),
        'rw_shift_mu': uni((nl, N_RW_IN), 0.0, 1.0),
        'rw_shift_mu_vres': uni((nl1, RW_MV_RANK), 0.0, 1.0),
        'rw_w0': uni((nl, RW_WIDTH), -6.5, -1.5),
        'rw_w2': nrm((nl, RW_DECAY_RANK, RW_WIDTH), RW_DECAY_RANK ** -0.5),
        'rw_a0': nrm((nl, RW_WIDTH), 0.1),
        'rw_a2': nrm((nl, RW_AAA_RANK, RW_WIDTH), RW_AAA_RANK ** -0.5),
        'rw_g2': nrm((nl, RW_GATE_RANK, RW_WIDTH), RW_GATE_RANK ** -0.5),
        'rw_v0': 1.0 + nrm((nl1, RW_WIDTH), 0.1),
        'rw_v2': nrm((nl1, RW_MV_RANK, RW_WIDTH), RW_MV_RANK ** -0.5),
        'rw_k_k': 0.85 + nrm((nl, RW_WIDTH), 0.05),
        'rw_k_a': 1.0 + nrm((nl, RW_WIDTH), 0.05),
        'rw_r_k': -0.04 + nrm((nl, RW_HEADS, RW_HEAD_DIM), 0.1),
        'rw_gn_w': 1.0 + nrm((nl, RW_WIDTH), 0.02),
        'rw_gn_b': nrm((nl, RW_WIDTH), 0.02),
        'mix_w_out': nrm((nl, MIX_WIDTH, D_MODEL), MIX_WIDTH ** -0.5 * DEEPNORM_BETA),
        'ln1_g': 1.0 + nrm((nl, D_MODEL), 0.02),
        'ln1_b': nrm((nl, D_MODEL), 0.02),
        'router_w': nrm((nl, D_MODEL, N_EXPERTS), D_MODEL ** -0.5),
        'router_b': nrm((nl, N_EXPERTS), 0.01),
        'exp_w_gate': nrm((nl, N_EXPERTS, D_MODEL, D_EXPERT), D_MODEL ** -0.5),
        'exp_b_gate': nrm((nl, N_EXPERTS, D_EXPERT), 0.02),
        'exp_w_up': nrm((nl, N_EXPERTS, D_MODEL, D_EXPERT), D_MODEL ** -0.5),
        'exp_b_up': nrm((nl, N_EXPERTS, D_EXPERT), 0.02),
        'exp_w_down': nrm((nl, N_EXPERTS, D_EXPERT, D_MODEL), D_EXPERT ** -0.5 * DEEPNORM_BETA),
        'exp_b_down': nrm((nl, N_EXPERTS, D_MODEL), 0.02),
        'ln2_g': 1.0 + nrm((nl, D_MODEL), 0.02),
        'ln2_b': nrm((nl, D_MODEL), 0.02),
    }


def reference(x, mix_w_in, mix_w_in_vres, dn_conv_w, dn_a_log, dn_dt_bias, dn_norm_w,
              rw_shift_mu, rw_shift_mu_vres, rw_w0, rw_w2, rw_a0, rw_a2, rw_g2, rw_v0, rw_v2,
              rw_k_k, rw_k_a, rw_r_k, rw_gn_w, rw_gn_b, mix_w_out, ln1_g, ln1_b,
              router_w, router_b, exp_w_gate, exp_b_gate, exp_w_up, exp_b_up,
              exp_w_down, exp_b_down, ln2_g, ln2_b):
    v_first = None
    for l in range(DEPTH):
        w_in, mu = mix_w_in[l], rw_shift_mu[l]
        v_mix = None
        if l > 0:
            w_in = jnp.concatenate([w_in, mix_w_in_vres[l - 1]], axis=1)
            mu = jnp.concatenate([mu, rw_shift_mu_vres[l - 1]])
            v_mix = (rw_v0[l - 1], rw_v2[l - 1])
        proj = jnp.einsum('bsd,dn->bsn', x, w_in)
        p_dn, p_rw = proj[..., :N_DN_IN], proj[..., N_DN_IN:]
        p_rw = p_rw + (token_shift(p_rw) - p_rw) * mu
        o_dn = gated_deltanet_group(p_dn, dn_conv_w[l], dn_a_log[l], dn_dt_bias[l], dn_norm_w[l])
        o_rw, v_first = rwkv7_group(p_rw, rw_w0[l], rw_w2[l], rw_a0[l], rw_a2[l], rw_g2[l],
                                    rw_k_k[l], rw_k_a[l], rw_r_k[l], rw_gn_w[l], rw_gn_b[l],
                                    v_first, v_mix)
        mix = jnp.concatenate([o_dn, o_rw], axis=-1).astype(x.dtype)
        x = layer_norm(DEEPNORM_ALPHA * x + mix @ mix_w_out[l], ln1_g[l], ln1_b[l])
        ffn = moe_ffn(x, router_w[l], router_b[l], exp_w_gate[l], exp_b_gate[l], exp_w_up[l],
                      exp_b_up[l], exp_w_down[l], exp_b_down[l])
        x = layer_norm(DEEPNORM_ALPHA * x + ffn, ln2_g[l], ln2_b[l])
    return x
```

```python
import functools
import math

import jax
import jax.numpy as jnp
from jax import lax
from jax.experimental import pallas as pl
from jax.experimental.pallas import tpu as pltpu

F32 = jnp.float32
BF16 = jnp.bfloat16
I32 = jnp.int32

LANES = 128
SUBLANES = 8
VMEM_LIMIT = 56 * 1024 * 1024

CHUNK = 64
DN_HEAD_DIM = 128
DN_CONV = 4
DN_NORM_EPS = 1e-6
RW_HEAD_DIM = 64
RW_GN_EPS = 64e-5
TOP_K = 4
SWIGLU_ALPHA = 1.702
SWIGLU_LIMIT = 7.0
LN_EPS = 1e-5
NEG_BIG = -1e30

HIGHEST = lax.Precision.HIGHEST


def _dot(a, b):
    return jnp.dot(a.astype(BF16), b.astype(BF16), preferred_element_type=F32)


def _dot_nt(a, b):
    return lax.dot_general(a.astype(BF16), b.astype(BF16), (((1,), (1,)), ((), ())),
                           preferred_element_type=F32)


def _dot_tn(a, b):
    return lax.dot_general(a.astype(BF16), b.astype(BF16), (((0,), (0,)), ((), ())),
                           preferred_element_type=F32)


def _dot_f32(a, b):
    return jnp.dot(a, b, preferred_element_type=F32, precision=HIGHEST)


def _iota2(shape, dim):
    return lax.broadcasted_iota(I32, shape, dim)


def _sigmoid(x):
    return 1.0 / (1.0 + jnp.exp(-x))


def _softplus(x):
    return jnp.maximum(x, 0.0) + jnp.log(1.0 + jnp.exp(-jnp.abs(x)))


def _silu(x):
    return x * _sigmoid(x)


def _unit_lower_inverse(low, n, blk):
    row = _iota2((n, n), 0)
    col = _iota2((n, n), 1)
    eye = (row == col).astype(F32)
    inv = eye - jnp.where((row >> 1) == (col >> 1), low, 0.0)
    shift = 1
    while (1 << shift) < blk:
        same_big = (row >> (shift + 1)) == (col >> (shift + 1))
        diff_small = (row >> shift) != (col >> shift)
        cm = jnp.where(jnp.logical_and(same_big, diff_small), low, 0.0)
        inv = inv - _dot(_dot(inv, cm), inv)
        shift += 1
    return inv


def _cparams(sem):
    return pltpu.CompilerParams(dimension_semantics=sem, vmem_limit_bytes=VMEM_LIMIT)


def _mm_kernel(x_ref, w_ref, o_ref):
    o_ref[...] = jnp.dot(x_ref[...], w_ref[...], preferred_element_type=F32).astype(o_ref.dtype)


def _mm_res_kernel(alpha, x_ref, w_ref, r_ref, o_ref):
    acc = jnp.dot(x_ref[...], w_ref[...], preferred_element_type=F32)
    o_ref[...] = (alpha * r_ref[...] + acc).astype(o_ref.dtype)


def _matmul(x, w, tm, tn, out_dtype, res=None, alpha=1.0, name="matmul"):
    m, k = x.shape
    n = w.shape[1]
    tm = min(tm, m)
    tn = max(t for t in range(LANES, min(tn, n) + 1, LANES) if n % t == 0)
    assert m % tm == 0 and n % LANES == 0
    in_specs = [pl.BlockSpec((tm, k), lambda i, j: (i, 0)),
                pl.BlockSpec((k, tn), lambda i, j: (0, j))]
    args = [x, w]
    body = _mm_kernel
    if res is not None:
        in_specs.append(pl.BlockSpec((tm, tn), lambda i, j: (i, j)))
        args.append(res)
        body = functools.partial(_mm_res_kernel, alpha)
    return pl.pallas_call(
        body, grid=(m // tm, n // tn), in_specs=in_specs,
        out_specs=pl.BlockSpec((tm, tn), lambda i, j: (i, j)),
        out_shape=jax.ShapeDtypeStruct((m, n), out_dtype),
        compiler_params=_cparams(("arbitrary", "arbitrary")), name=name)(*args)


def _dn_kernel(q_ref, k_ref, v_ref, z_ref, ba_ref, cwq_ref, cwk_ref, cwv_ref, alog_ref, dtb_ref,
               nw_ref, o_ref, qbuf, kbuf, vbuf, s_ref):
    c_len, d = CHUNK, DN_HEAD_DIM
    h = pl.program_id(1)
    c = pl.program_id(2)
    halo = SUBLANES

    @pl.when(c == 0)
    def _():
        s_ref[...] = jnp.zeros_like(s_ref)
        for buf in (qbuf, kbuf, vbuf):
            buf[0:halo, :] = jnp.zeros((halo, d), F32)

    @pl.when(c > 0)
    def _():
        for buf in (qbuf, kbuf, vbuf):
            buf[0:halo, :] = buf[c_len:c_len + halo, :]

    def conv_silu(buf, x_ref, cw_ref):
        buf[halo:halo + c_len, :] = x_ref[...]
        acc = jnp.zeros((c_len, d), F32)
        for j in range(DN_CONV):
            off = halo - (DN_CONV - 1) + j
            acc = acc + cw_ref[j:j + 1, :] * buf[off:off + c_len, :]
        return _silu(acc)

    q = conv_silu(qbuf, q_ref, cwq_ref)
    k = conv_silu(kbuf, k_ref, cwk_ref)
    v = conv_silu(vbuf, v_ref, cwv_ref)
    q = q * lax.rsqrt(jnp.sum(q * q, axis=1, keepdims=True) + 1e-6) * (d ** -0.5)
    k = k * lax.rsqrt(jnp.sum(k * k, axis=1, keepdims=True) + 1e-6)

    ba = ba_ref[...]
    nh = pl.num_programs(1)
    lane = _iota2((c_len, LANES), 1)
    row64 = _iota2((c_len, c_len), 0)
    col64 = _iota2((c_len, c_len), 1)
    tri_incl = (row64 >= col64).astype(F32)
    g_all = -jnp.exp(alog_ref[...]) * _softplus(ba + dtb_ref[...])
    gc_all = _dot_f32(tri_incl, g_all)
    beta = _sigmoid(jnp.sum(jnp.where(lane == h, ba, 0.0), axis=1, keepdims=True))
    gc = jnp.sum(jnp.where(lane == nh + h, gc_all, 0.0), axis=1, keepdims=True)
    gc_last = gc[c_len - 1:c_len, :]

    gc_col = jnp.broadcast_to(gc, (c_len, c_len))
    gc_row = jnp.sum(jnp.where(row64 == col64, gc_col, 0.0), axis=0, keepdims=True)
    causal = row64 >= col64
    strict = row64 > col64
    decay = jnp.exp(jnp.where(causal, gc_col - gc_row, NEG_BIG))
    eg = jnp.exp(gc)

    kb = k * beta
    low = jnp.where(strict, _dot_nt(kb, k) * decay, 0.0)
    t_inv = _unit_lower_inverse(low, c_len, c_len)
    u = _dot(t_inv, v * beta)
    w = _dot(t_inv, kb * eg)
    qk = _dot_nt(q, k) * decay
    q_dec = q * eg
    k_dec = k * jnp.exp(gc_last - gc)

    s = s_ref[...]
    v_new = u - _dot(w, s)
    out = _dot(q_dec, s) + _dot(qk, v_new)
    s_ref[...] = s * jnp.exp(gc_last) + _dot_tn(k_dec, v_new)

    out = out * lax.rsqrt(jnp.mean(out * out, axis=1, keepdims=True) + DN_NORM_EPS)
    o_ref[...] = (out * nw_ref[...] * _silu(z_ref[...])).astype(o_ref.dtype)


def _gated_deltanet(proj, tail, ba_blk, conv_w, alog_pad, dtb_pad, norm_w, bsz, seq, n_heads):
    t_tot = bsz * seq
    nc = seq // CHUNK
    d = DN_HEAD_DIM

    def col(group):
        return pl.BlockSpec((CHUNK, d), lambda b, h, c, g=group: (b * nc + c, g * n_heads + h))

    def cw(group):
        return pl.BlockSpec((DN_CONV, d), lambda b, h, c, g=group: (0, g * n_heads + h))

    vec = pl.BlockSpec((1, LANES), lambda b, h, c: (0, 0))
    return pl.pallas_call(
        _dn_kernel, grid=(bsz, n_heads, nc),
        in_specs=[col(0), col(1), col(2), col(3),
                  pl.BlockSpec((CHUNK, LANES), lambda b, h, c: (b * nc + c, ba_blk)),
                  cw(0), cw(1), cw(2), vec, vec, vec],
        out_specs=pl.BlockSpec((CHUNK, d), lambda b, h, c: (b * nc + c, h)),
        out_shape=jax.ShapeDtypeStruct((t_tot, n_heads * d), BF16),
        scratch_shapes=[pltpu.VMEM((CHUNK + SUBLANES, d), F32)] * 3 + [pltpu.VMEM((d, d), F32)],
        compiler_params=_cparams(("arbitrary", "arbitrary", "arbitrary")), name="gated_deltanet",
    )(proj, proj, proj, proj, tail, conv_w, conv_w, conv_w, alog_pad, dtb_pad, norm_w)


def _rw_kernel(has_vres, *refs):
    if has_vres:
        (r_ref, k_ref, v_ref, lo_ref, vf_ref, mur_ref, muk_ref, muv_ref, mulo_ref, w0_ref, a0_ref,
         kk_ref, ka_ref, rk_ref, gw_ref, gb_ref, w2_ref, a2_ref, g2_ref, v0_ref, v2_ref,
         o_ref, rbuf, kbuf, vbuf, lobuf, ht_ref) = refs
    else:
        (r_ref, k_ref, v_ref, lo_ref, mur_ref, muk_ref, muv_ref, mulo_ref, w0_ref, a0_ref,
         kk_ref, ka_ref, rk_ref, gw_ref, gb_ref, w2_ref, a2_ref, g2_ref,
         o_ref, vf_out_ref, rbuf, kbuf, vbuf, lobuf, ht_ref) = refs
    c_len, n = CHUNK, RW_HEAD_DIM
    c = pl.program_id(2)
    halo = SUBLANES
    bufs = (rbuf, kbuf, vbuf, lobuf)

    @pl.when(c == 0)
    def _():
        ht_ref[...] = jnp.zeros_like(ht_ref)
        for buf in bufs:
            buf[0:halo, :] = jnp.zeros((halo, buf.shape[1]), F32)

    @pl.when(c > 0)
    def _():
        for buf in bufs:
            buf[0:halo, :] = buf[c_len:c_len + halo, :]

    def shift_mix(buf, x_ref, mu_ref):
        x = x_ref[...]
        buf[halo:halo + c_len, :] = x
        prev = buf[halo - 1:halo - 1 + c_len, :]
        return x + (prev - x) * mu_ref[...]

    r = shift_mix(rbuf, r_ref, mur_ref)
    k = shift_mix(kbuf, k_ref, muk_ref)
    v = shift_mix(vbuf, v_ref, muv_ref)
    lo = shift_mix(lobuf, lo_ref, mulo_ref)
    w_pre = w0_ref[...] + _dot(jnp.tanh(lo[:, 0:128]), w2_ref[...])
    w_log = -_softplus(-w_pre) - 0.5
    logd = -jnp.exp(w_log)
    a = _sigmoid(a0_ref[...] + _dot(lo[:, 128:256], a2_ref[...]))
    g = _dot(_sigmoid(lo[:, 256:512]), g2_ref[...])
    if has_vres:
        v = v + (vf_ref[...] - v) * _sigmoid(v0_ref[...] + _dot(lo[:, 512:640], v2_ref[...]))
    else:
        vf_out_ref[...] = v

    r128 = _iota2((LANES, LANES), 0)
    c128 = _iota2((LANES, LANES), 1)
    seg_ones = ((r128 >> 6) == (c128 >> 6)).astype(F32)

    def seg_sum(x):
        return _dot_f32(x, seg_ones)

    kk = k * kk_ref[...]
    kk = kk * lax.rsqrt(seg_sum(kk * kk) + 1e-12)
    k = k * (1.0 + (a - 1.0) * ka_ref[...])
    a_vec = -kk
    b_vec = kk * a

    row64 = _iota2((c_len, c_len), 0)
    col64 = _iota2((c_len, c_len), 1)
    tri_incl = (row64 >= col64).astype(F32)
    g_cum = _dot_f32(tri_incl, logd)
    p_incl = jnp.exp(g_cum)
    p_prev = jnp.exp(g_cum - logd)
    p_inv = jnp.exp(-g_cum)
    p_last = p_incl[c_len - 1:c_len, :]

    lane_tok = _iota2((c_len, LANES), 1)
    head0 = lane_tok < n

    def stack(x):
        return jnp.concatenate([jnp.where(head0, x, 0.0), jnp.where(head0, 0.0, x)], axis=0)

    a_s = stack(a_vec * p_prev)
    r_s = stack(r * p_incl)
    b_s = stack(b_vec * p_inv)
    k_s = stack(k * p_inv)
    v_s = stack(v)

    strict = r128 > c128
    incl = r128 >= c128
    a_ab = _dot_nt(a_s, b_s)
    a_ak = _dot_nt(a_s, k_s)
    a_rb = _dot_nt(r_s, b_s)
    a_rk = _dot_nt(r_s, k_s)
    t_inv = _unit_lower_inverse(jnp.where(strict, -a_ab, 0.0), 2 * c_len, c_len)
    u0 = _dot(t_inv, _dot(jnp.where(strict, a_ak, 0.0), v_s))
    w = _dot(t_inv, a_s)

    ht = ht_ref[...]
    u = _dot_nt(w, ht) + u0
    y_s = _dot_nt(r_s, ht) + _dot(jnp.where(incl, a_rb, 0.0), u) + _dot(jnp.where(incl, a_rk, 0.0), v_s)
    y = y_s[0:c_len, :] + y_s[c_len:2 * c_len, :]
    same_head = (r128 >> 6) == (c128 >> 6)
    ht_new = ht + _dot_tn(u, b_s) + _dot_tn(v_s, k_s)
    ht_ref[...] = jnp.where(same_head, ht_new * p_last, 0.0)

    mu = seg_sum(y) * (1.0 / n)
    yc = y - mu
    var = seg_sum(yc * yc) * (1.0 / n)
    yn = yc * lax.rsqrt(var + RW_GN_EPS) * gw_ref[...] + gb_ref[...]
    bonus = seg_sum(r * k * rk_ref[...]) * v
    o_ref[...] = ((yn + bonus) * g).astype(o_ref.dtype)


def _rwkv7(proj, tail, r_blk0, v_first, prm, bsz, seq, n_pairs):
    t_tot = bsz * seq
    nc = seq // CHUNK
    w_rw = n_pairs * LANES
    has_vres = v_first is not None

    def col(group):
        return pl.BlockSpec((CHUNK, LANES), lambda b, p, c, g=group: (b * nc + c, r_blk0 + g * n_pairs + p))

    tok = pl.BlockSpec((CHUNK, LANES), lambda b, p, c: (b * nc + c, p))
    lo_spec = pl.BlockSpec((CHUNK, 5 * LANES), lambda b, p, c: (b * nc + c, 0))
    vec = pl.BlockSpec((1, LANES), lambda b, p, c: (0, p))
    vec_lo = pl.BlockSpec((1, 5 * LANES), lambda b, p, c: (0, 0))

    def lowrank(rank):
        return pl.BlockSpec((rank, LANES), lambda b, p, c: (0, p))

    in_specs = [col(0), col(1), col(2), lo_spec]
    args = [proj, proj, proj, tail]
    if has_vres:
        in_specs.append(tok)
        args.append(v_first)
    in_specs += [vec, vec, vec, vec_lo] + [vec] * 7 + [lowrank(128), lowrank(128), lowrank(256)]
    args += [prm["mu_r"], prm["mu_k"], prm["mu_v"], prm["mu_lo"], prm["w0"], prm["a0"], prm["k_k"],
             prm["k_a"], prm["r_k"], prm["gn_w"], prm["gn_b"], prm["w2"], prm["a2"], prm["g2"]]
    if has_vres:
        in_specs += [vec, lowrank(128)]
        args += [prm["v0"], prm["v2"]]
    out_shape = [jax.ShapeDtypeStruct((t_tot, w_rw), BF16)]
    out_specs = [tok]
    if not has_vres:
        out_shape.append(jax.ShapeDtypeStruct((t_tot, w_rw), F32))
        out_specs.append(tok)
    halo_rows = CHUNK + SUBLANES
    outs = pl.pallas_call(
        functools.partial(_rw_kernel, has_vres), grid=(bsz, n_pairs, nc),
        in_specs=in_specs, out_specs=out_specs, out_shape=out_shape,
        scratch_shapes=[pltpu.VMEM((halo_rows, LANES), F32)] * 3
        + [pltpu.VMEM((halo_rows, 5 * LANES), F32), pltpu.VMEM((LANES, LANES), F32)],
        compiler_params=_cparams(("arbitrary", "arbitrary", "arbitrary")), name="rwkv7",
    )(*args)
    if has_vres:
        return outs[0], v_first
    return outs[0], outs[1]


def _layer_norm_rows(y, g, b):
    mu = jnp.mean(y, axis=1, keepdims=True)
    yc = y - mu
    var = jnp.mean(yc * yc, axis=1, keepdims=True)
    return yc * lax.rsqrt(var + LN_EPS) * g + b


def _ln_router_kernel(y_ref, g_ref, b_ref, rw_ref, rb_ref, x_ref, ti_ref, tw_ref, pos_ref, cnt_ref,
                      base_ref):
    tm = y_ref.shape[0]
    i = pl.program_id(0)

    @pl.when(i == 0)
    def _():
        base_ref[...] = jnp.zeros_like(base_ref)

    x = _layer_norm_rows(y_ref[...], g_ref[...], b_ref[...])
    x_ref[...] = x
    logits = _dot(x, rw_ref[...]) + rb_ref[...]
    lane = _iota2((tm, LANES), 1)
    lane_f = lane.astype(F32)
    work = logits
    sels, vals, idxs = [], [], []
    for _ in range(TOP_K):
        m = jnp.max(work, axis=1, keepdims=True)
        idx = jnp.min(jnp.where(work == m, lane_f, float(LANES)), axis=1, keepdims=True)
        sel = lane_f == idx
        sels.append(sel)
        vals.append(m)
        idxs.append(idx)
        work = jnp.where(sel, 2.0 * NEG_BIG, work)
    exps = [jnp.exp(m - vals[0]) for m in vals]
    denom = exps[0] + exps[1] + exps[2] + exps[3]
    cnt = jnp.zeros((tm, LANES), F32)
    for sel in sels:
        cnt = cnt + sel.astype(F32)
    rowt = _iota2((tm, tm), 0)
    colt = _iota2((tm, tm), 1)
    before = _dot((rowt > colt).astype(F32), cnt)
    posmat = base_ref[...] + before
    ti = jnp.zeros((tm, LANES), F32)
    tw = jnp.zeros((tm, LANES), F32)
    pos = jnp.zeros((tm, LANES), F32)
    for kk in range(TOP_K):
        here = lane == kk
        ti = jnp.where(here, idxs[kk], ti)
        tw = jnp.where(here, exps[kk] / denom, tw)
        pos = jnp.where(here, jnp.sum(jnp.where(sels[kk], posmat, 0.0), axis=1, keepdims=True), pos)
    ti_ref[...] = ti.astype(I32)
    tw_ref[...] = tw
    pos_ref[...] = pos.astype(I32)
    base_new = base_ref[...] + jnp.sum(cnt, axis=0, keepdims=True)
    base_ref[...] = base_new
    cnt_ref[...] = jnp.broadcast_to(base_new, cnt_ref.shape).astype(I32)


def _ln_router(y, g, b, rw_pad, rb_pad, tm):
    t_tot, dm = y.shape
    tm = min(tm, t_tot)
    assert t_tot % tm == 0
    row = pl.BlockSpec((tm, dm), lambda i: (i, 0))
    vec = pl.BlockSpec((1, dm), lambda i: (0, 0))
    small = pl.BlockSpec((tm, LANES), lambda i: (i, 0))
    return pl.pallas_call(
        _ln_router_kernel, grid=(t_tot // tm,),
        in_specs=[row, vec, vec, pl.BlockSpec((dm, LANES), lambda i: (0, 0)),
                  pl.BlockSpec((1, LANES), lambda i: (0, 0))],
        out_specs=[row, small, small, small, pl.BlockSpec((SUBLANES, LANES), lambda i: (0, 0))],
        out_shape=[jax.ShapeDtypeStruct((t_tot, dm), F32),
                   jax.ShapeDtypeStruct((t_tot, LANES), I32),
                   jax.ShapeDtypeStruct((t_tot, LANES), F32),
                   jax.ShapeDtypeStruct((t_tot, LANES), I32),
                   jax.ShapeDtypeStruct((SUBLANES, LANES), I32)],
        scratch_shapes=[pltpu.VMEM((1, LANES), F32)],
        compiler_params=_cparams(("arbitrary",)), name="ln_router",
    )(y, g, b, rw_pad, rb_pad)


def _row_gather_start(src_hbm, dst_buf, sem, idx_of_row, n_rows):
    def body(r, carry):
        pltpu.make_async_copy(src_hbm.at[pl.ds(idx_of_row(r), 1)], dst_buf.at[pl.ds(r, 1)], sem).start()
        return carry
    lax.fori_loop(0, n_rows, body, 0)


def _row_gather_wait(src_hbm, dst_buf, sem, n_rows):
    def body(r, carry):
        pltpu.make_async_copy(src_hbm.at[pl.ds(0, 1)], dst_buf.at[pl.ds(r, 1)], sem).wait()
        return carry
    lax.fori_loop(0, n_rows, body, 0)


def _moe_up_kernel(te_ref, nu_ref, tos_ref, x_hbm, wg_ref, wu_ref, bg_ref, bu_ref, h_ref, xbuf, sem):
    tm = h_ref.shape[0]
    i = pl.program_id(0)
    slot = lax.rem(i, 2)
    n_used = nu_ref[0]

    def start(tile, s):
        _row_gather_start(x_hbm, xbuf.at[s], sem.at[s], lambda r: tos_ref[tile * tm + r], tm)

    @pl.when(i == 0)
    def _():
        start(0, 0)

    @pl.when(i + 1 < n_used)
    def _():
        start(i + 1, 1 - slot)

    @pl.when(i < n_used)
    def _():
        _row_gather_wait(x_hbm, xbuf.at[slot], sem.at[slot], tm)
        x = xbuf[slot].astype(BF16)
        gate = jnp.dot(x, wg_ref[...], preferred_element_type=F32) + bg_ref[...]
        up = jnp.dot(x, wu_ref[...], preferred_element_type=F32) + bu_ref[...]
        gate = jnp.minimum(gate, SWIGLU_LIMIT)
        up = jnp.clip(up, -SWIGLU_LIMIT, SWIGLU_LIMIT)
        h_ref[...] = ((up + 1.0) * gate * _sigmoid(gate * SWIGLU_ALPHA)).astype(h_ref.dtype)

    @pl.when(i >= n_used)
    def _():
        h_ref[...] = jnp.zeros_like(h_ref)


def _moe_down_kernel(te_ref, nu_ref, h_ref, wd_ref, bd_ref, y_ref):
    i = pl.program_id(0)

    @pl.when(i < nu_ref[0])
    def _():
        y_ref[...] = (jnp.dot(h_ref[...], wd_ref[...], preferred_element_type=F32) + bd_ref[...]).astype(y_ref.dtype)

    @pl.when(i >= nu_ref[0])
    def _():
        y_ref[...] = jnp.zeros_like(y_ref)


def _moe_experts(x, tile_expert, n_used, tok_of_slot, wg, wu, bg, bu, wd, bd, tm):
    n_slots = tok_of_slot.shape[0]
    n_tiles = n_slots // tm
    n_exp, dm, de = wg.shape
    h = pl.pallas_call(
        _moe_up_kernel,
        grid_spec=pltpu.PrefetchScalarGridSpec(
            num_scalar_prefetch=3, grid=(n_tiles,),
            in_specs=[pl.BlockSpec(memory_space=pl.ANY),
                      pl.BlockSpec((None, dm, de), lambda i, te, nu, tos: (te[i], 0, 0)),
                      pl.BlockSpec((None, dm, de), lambda i, te, nu, tos: (te[i], 0, 0)),
                      pl.BlockSpec((None, 1, de), lambda i, te, nu, tos: (te[i], 0, 0)),
                      pl.BlockSpec((None, 1, de), lambda i, te, nu, tos: (te[i], 0, 0))],
            out_specs=pl.BlockSpec((tm, de), lambda i, te, nu, tos: (i, 0)),
            scratch_shapes=[pltpu.VMEM((2, tm, dm), F32), pltpu.SemaphoreType.DMA((2,))]),
        out_shape=jax.ShapeDtypeStruct((n_slots, de), BF16),
        compiler_params=_cparams(("arbitrary",)), name="moe_gate_up",
    )(tile_expert, n_used, tok_of_slot, x, wg, wu, bg, bu)
    return pl.pallas_call(
        _moe_down_kernel,
        grid_spec=pltpu.PrefetchScalarGridSpec(
            num_scalar_prefetch=2, grid=(n_tiles,),
            in_specs=[pl.BlockSpec((tm, de), lambda i, te, nu: (i, 0)),
                      pl.BlockSpec((None, de, dm), lambda i, te, nu: (te[i], 0, 0)),
                      pl.BlockSpec((None, 1, dm), lambda i, te, nu: (te[i], 0, 0))],
            out_specs=pl.BlockSpec((tm, dm), lambda i, te, nu: (i, 0))),
        out_shape=jax.ShapeDtypeStruct((n_slots, dm), F32),
        compiler_params=_cparams(("arbitrary",)), name="moe_down",
    )(tile_expert, n_used, h, wd, bd)


def _combine_ln_kernel(alpha, dest_ref, ys_hbm, x_ref, tw_ref, g_ref, b_ref, o_ref, ob_ref, ybuf, sem):
    tm = x_ref.shape[0]
    i = pl.program_id(0)
    n = pl.num_programs(0)
    slot = lax.rem(i, 2)

    def start(tile, s):
        for kk in range(TOP_K):
            _row_gather_start(ys_hbm, ybuf.at[s, kk], sem.at[s],
                              lambda r, kk=kk: dest_ref[(tile * tm + r) * TOP_K + kk], tm)

    @pl.when(i == 0)
    def _():
        start(0, 0)

    @pl.when(i + 1 < n)
    def _():
        start(i + 1, 1 - slot)

    for kk in range(TOP_K):
        _row_gather_wait(ys_hbm, ybuf.at[slot, kk], sem.at[slot], tm)
    tw = tw_ref[...]
    acc = alpha * x_ref[...]
    for kk in range(TOP_K):
        acc = acc + tw[:, kk:kk + 1] * ybuf[slot, kk]
    out = _layer_norm_rows(acc, g_ref[...], b_ref[...])
    o_ref[...] = out
    ob_ref[...] = out.astype(BF16)


def _combine_ln(ys, x, tw, dest_flat, g, b, alpha, tm):
    t_tot, dm = x.shape
    tm = min(tm, t_tot)
    assert t_tot % tm == 0
    row = pl.BlockSpec((tm, dm), lambda i, d: (i, 0))
    vec = pl.BlockSpec((1, dm), lambda i, d: (0, 0))
    return pl.pallas_call(
        functools.partial(_combine_ln_kernel, alpha),
        grid_spec=pltpu.PrefetchScalarGridSpec(
            num_scalar_prefetch=1, grid=(t_tot // tm,),
            in_specs=[pl.BlockSpec(memory_space=pl.ANY), row,
                      pl.BlockSpec((tm, LANES), lambda i, d: (i, 0)), vec, vec],
            out_specs=[row, row],
            scratch_shapes=[pltpu.VMEM((2, TOP_K, tm, dm), F32), pltpu.SemaphoreType.DMA((2,))]),
        out_shape=[jax.ShapeDtypeStruct((t_tot, dm), F32), jax.ShapeDtypeStruct((t_tot, dm), BF16)],
        compiler_params=_cparams(("arbitrary",)), name="moe_combine_ln",
    )(dest_flat, ys, x, tw, g, b)


def _tiles(t_tot):
    return dict(mm_m=min(1024, t_tot), mm_n=1024, ln_m=min(256, t_tot), moe_m=min(512, t_tot),
                comb_m=min(64, t_tot))


def _pad_cols(w, width):
    return jnp.pad(w, ((0, 0), (0, width - w.shape[1])))


def _pad_rows(w, rows):
    return jnp.pad(w, ((0, rows - w.shape[0]), (0, 0)))


def kernel(x, mix_w_in, mix_w_in_vres, dn_conv_w, dn_a_log, dn_dt_bias, dn_norm_w, rw_shift_mu, rw_shift_mu_vres, rw_w0, rw_w2, rw_a0, rw_a2, rw_g2, rw_v0, rw_v2, rw_k_k, rw_k_a, rw_r_k, rw_gn_w, rw_gn_b, mix_w_out, ln1_g, ln1_b, router_w, router_b, exp_w_gate, exp_b_gate, exp_w_up, exp_b_up, exp_w_down, exp_b_down, ln2_g, ln2_b):
    bsz, seq, dm = x.shape
    depth = mix_w_in.shape[0]
    t_tot = bsz * seq
    dn_heads = dn_a_log.shape[1]
    w_dn = dn_heads * DN_HEAD_DIM
    w_rw = rw_w0.shape[1]
    n_pairs = w_rw // LANES
    r_dec, r_aaa, r_gate, r_mv = rw_w2.shape[1], rw_a2.shape[1], rw_g2.shape[1], rw_v2.shape[1]
    n_exp = router_w.shape[2]
    assert seq % CHUNK == 0 and w_rw % LANES == 0 and 2 * dn_heads <= LANES
    assert max(r_dec, r_aaa, r_mv) <= LANES and r_gate == 2 * LANES and n_exp <= LANES
    alpha = (2 * depth) ** 0.25
    tl = _tiles(t_tot)
    n_dn_in = 4 * w_dn + 2 * dn_heads

    x2d = x.reshape(t_tot, dm)
    xb = x2d.astype(BF16)
    v_first = None
    for l in range(depth):
        w_in = mix_w_in[l]
        rw0 = n_dn_in
        o_lo = rw0 + 3 * w_rw
        w_main = jnp.concatenate([w_in[:, :4 * w_dn], w_in[:, rw0:o_lo]], axis=1).astype(BF16)
        v_lo_w = mix_w_in_vres[l - 1] if l > 0 else jnp.zeros((dm, r_mv), F32)
        w_tail = jnp.concatenate([
            _pad_cols(w_in[:, o_lo:o_lo + r_dec], LANES),
            _pad_cols(w_in[:, o_lo + r_dec:o_lo + r_dec + r_aaa], LANES),
            w_in[:, o_lo + r_dec + r_aaa:o_lo + r_dec + r_aaa + r_gate],
            _pad_cols(v_lo_w, LANES),
            _pad_cols(w_in[:, 4 * w_dn:n_dn_in], LANES)], axis=1).astype(BF16)
        proj = _matmul(xb, w_main, tl["mm_m"], tl["mm_n"], F32, name="in_proj")
        tail = _matmul(xb, w_tail, tl["mm_m"], w_tail.shape[1], F32, name="in_proj_tail")

        zeros_h = jnp.zeros((dn_heads,), F32)
        alog_pad = _pad_cols(jnp.concatenate([zeros_h, dn_a_log[l]])[None, :], LANES)
        dtb_pad = _pad_cols(jnp.concatenate([zeros_h, dn_dt_bias[l]])[None, :], LANES)
        o_dn = _gated_deltanet(proj, tail, 5, dn_conv_w[l], alog_pad, dtb_pad, dn_norm_w[l][None, :],
                               bsz, seq, dn_heads)

        mu = rw_shift_mu[l]
        mu_lo_parts = [_pad_cols(mu[None, 3 * w_rw:3 * w_rw + r_dec], LANES),
                       _pad_cols(mu[None, 3 * w_rw + r_dec:3 * w_rw + r_dec + r_aaa], LANES),
                       mu[None, 3 * w_rw + r_dec + r_aaa:],
                       _pad_cols(rw_shift_mu_vres[l - 1][None, :] if l > 0 else jnp.zeros((1, r_mv), F32), LANES)]
        prm = dict(mu_r=mu[None, :w_rw], mu_k=mu[None, w_rw:2 * w_rw], mu_v=mu[None, 2 * w_rw:3 * w_rw],
                   mu_lo=jnp.concatenate(mu_lo_parts, axis=1),
                   w0=rw_w0[l][None, :], a0=rw_a0[l][None, :], k_k=rw_k_k[l][None, :], k_a=rw_k_a[l][None, :],
                   r_k=rw_r_k[l].reshape(1, w_rw), gn_w=rw_gn_w[l][None, :], gn_b=rw_gn_b[l][None, :],
                   w2=_pad_rows(rw_w2[l], LANES), a2=_pad_rows(rw_a2[l], LANES), g2=rw_g2[l])
        if l > 0:
            prm["v0"] = rw_v0[l - 1][None, :]
            prm["v2"] = _pad_rows(rw_v2[l - 1], LANES)
        o_rw, v_first = _rwkv7(proj, tail, 4 * dn_heads, v_first if l > 0 else None, prm, bsz, seq, n_pairs)

        mix = jnp.concatenate([o_dn, o_rw], axis=1)
        y1 = _matmul(mix, mix_w_out[l].astype(BF16), tl["mm_m"], tl["mm_n"], F32, res=x2d, alpha=alpha,
                     name="out_proj")
        rw_pad = _pad_cols(router_w[l], LANES)
        rb_pad = jnp.concatenate([router_b[l], jnp.full((LANES - n_exp,), NEG_BIG, F32)])[None, :]
        x1, top_i, top_w, pos, cnt = _ln_router(y1, ln1_g[l][None, :], ln1_b[l][None, :], rw_pad, rb_pad,
                                                tl["ln_m"])

        tm = tl["moe_m"]
        counts = cnt[0, :n_exp]
        padded = ((counts + tm - 1) // tm) * tm
        ends = jnp.cumsum(padded)
        offs = ends - padded
        n_tiles = (t_tot * TOP_K) // tm + n_exp
        dest = offs[top_i[:, :TOP_K]] + pos[:, :TOP_K]
        tok_ids = jnp.broadcast_to(jnp.arange(t_tot, dtype=I32)[:, None], (t_tot, TOP_K))
        tok_of_slot = jnp.zeros((n_tiles * tm,), I32).at[dest.reshape(-1)].set(tok_ids.reshape(-1))
        tile_start = jnp.arange(n_tiles, dtype=I32) * tm
        tile_expert = jnp.minimum(jnp.searchsorted(ends, tile_start, side="right"), n_exp - 1).astype(I32)
        n_used = (ends[-1] // tm).astype(I32)[None]

        ys = _moe_experts(x1, tile_expert, n_used, tok_of_slot,
                          exp_w_gate[l].astype(BF16), exp_w_up[l].astype(BF16),
                          exp_b_gate[l][:, None, :], exp_b_up[l][:, None, :],
                          exp_w_down[l].astype(BF16), exp_b_down[l][:, None, :], tm)
        x2d, xb = _combine_ln(ys, x1, top_w, dest.reshape(-1), ln2_g[l][None, :], ln2_b[l][None, :],
                              alpha, tl["comb_m"])
    return x2d.reshape(bsz, seq, dm)
```

```python
import functools
import math

import jax
import jax.numpy as jnp
from jax import lax
from jax.experimental import pallas as pl
from jax.experimental.pallas import tpu as pltpu

F32 = jnp.float32
BF16 = jnp.bfloat16
I32 = jnp.int32

LANES = 128
SUBLANES = 8
VMEM_LIMIT = 56 * 1024 * 1024

CHUNK = 64
DN_HEAD_DIM = 128
DN_CONV = 4
DN_NORM_EPS = 1e-6
RW_HEAD_DIM = 64
RW_GN_EPS = 64e-5
TOP_K = 4
SWIGLU_ALPHA = 1.702
SWIGLU_LIMIT = 7.0
LN_EPS = 1e-5
NEG_BIG = -1e30

HIGHEST = lax.Precision.HIGHEST


def _dot(a, b):
    return jnp.dot(a.astype(BF16), b.astype(BF16), preferred_element_type=F32)


def _dot_nt(a, b):
    return lax.dot_general(a.astype(BF16), b.astype(BF16), (((1,), (1,)), ((), ())),
                           preferred_element_type=F32)


def _dot_tn(a, b):
    return lax.dot_general(a.astype(BF16), b.astype(BF16), (((0,), (0,)), ((), ())),
                           preferred_element_type=F32)


def _dot_f32(a, b):
    return jnp.dot(a, b, preferred_element_type=F32, precision=HIGHEST)


def _iota2(shape, dim):
    return lax.broadcasted_iota(I32, shape, dim)


def _sigmoid(x):
    return 1.0 / (1.0 + jnp.exp(-x))


def _softplus(x):
    return jnp.maximum(x, 0.0) + jnp.log(1.0 + jnp.exp(-jnp.abs(x)))


def _silu(x):
    return x * _sigmoid(x)


def _each(fn, *lists):
    return [fn(*args) for args in zip(*lists)]


def _unit_lower_inverse(lows, n, blk):
    row = _iota2((n, n), 0)
    col = _iota2((n, n), 1)
    eye = (row == col).astype(F32)
    pair = (row >> 1) == (col >> 1)
    invs = _each(lambda low: eye - jnp.where(pair, low, 0.0), lows)
    shift = 1
    while (1 << shift) < blk:
        same_big = (row >> (shift + 1)) == (col >> (shift + 1))
        diff_small = (row >> shift) != (col >> shift)
        mask = jnp.logical_and(same_big, diff_small)
        cms = _each(lambda low: jnp.where(mask, low, 0.0), lows)
        tmp = _each(_dot, invs, cms)
        upd = _each(_dot, tmp, invs)
        invs = _each(lambda inv, d: inv - d, invs, upd)
        shift += 1
    return invs


def _cparams(sem):
    return pltpu.CompilerParams(dimension_semantics=sem, vmem_limit_bytes=VMEM_LIMIT)


def _mm_kernel(x_ref, w_ref, o_ref):
    o_ref[...] = jnp.dot(x_ref[...], w_ref[...], preferred_element_type=F32).astype(o_ref.dtype)


def _mm_res_kernel(alpha, x_ref, w_ref, r_ref, o_ref):
    acc = jnp.dot(x_ref[...], w_ref[...], preferred_element_type=F32)
    o_ref[...] = (alpha * r_ref[...] + acc).astype(o_ref.dtype)


def _matmul(x, w, tm, tn, out_dtype, res=None, alpha=1.0, name="matmul"):
    m, k = x.shape
    n = w.shape[1]
    tm = min(tm, m)
    tn = max(t for t in range(LANES, min(tn, n) + 1, LANES) if n % t == 0)
    assert m % tm == 0 and n % LANES == 0
    in_specs = [pl.BlockSpec((tm, k), lambda i, j: (i, 0)),
                pl.BlockSpec((k, tn), lambda i, j: (0, j))]
    args = [x, w]
    body = _mm_kernel
    if res is not None:
        in_specs.append(pl.BlockSpec((tm, tn), lambda i, j: (i, j)))
        args.append(res)
        body = functools.partial(_mm_res_kernel, alpha)
    return pl.pallas_call(
        body, grid=(m // tm, n // tn), in_specs=in_specs,
        out_specs=pl.BlockSpec((tm, tn), lambda i, j: (i, j)),
        out_shape=jax.ShapeDtypeStruct((m, n), out_dtype),
        compiler_params=_cparams(("arbitrary", "arbitrary")), name=name)(*args)


def _dn_kernel(q_ref, k_ref, v_ref, z_ref, ba_ref, cwq_ref, cwk_ref, cwv_ref, alog_ref, dtb_ref,
               nw_ref, o_ref, qbuf, kbuf, vbuf, s_ref):
    c_len, d = CHUNK, DN_HEAD_DIM
    n_grp = s_ref.shape[0]
    gw = n_grp * d
    hg = pl.program_id(1)
    c = pl.program_id(2)
    halo = SUBLANES

    @pl.when(c == 0)
    def _():
        s_ref[...] = jnp.zeros_like(s_ref)
        for buf in (qbuf, kbuf, vbuf):
            buf[0:halo, :] = jnp.zeros((halo, gw), F32)

    @pl.when(c > 0)
    def _():
        for buf in (qbuf, kbuf, vbuf):
            buf[0:halo, :] = buf[c_len:c_len + halo, :]

    def conv_silu(buf, x_ref, cw_ref):
        buf[halo:halo + c_len, :] = x_ref[...]
        acc = jnp.zeros((c_len, gw), F32)
        for j in range(DN_CONV):
            off = halo - (DN_CONV - 1) + j
            acc = acc + cw_ref[j:j + 1, :] * buf[off:off + c_len, :]
        return _silu(acc)

    q_all = conv_silu(qbuf, q_ref, cwq_ref)
    k_all = conv_silu(kbuf, k_ref, cwk_ref)
    v_all = conv_silu(vbuf, v_ref, cwv_ref)
    gate_all = nw_ref[...] * _silu(z_ref[...])

    ba = ba_ref[...]
    nh = pl.num_programs(1) * n_grp
    lane = _iota2((c_len, LANES), 1)
    row64 = _iota2((c_len, c_len), 0)
    col64 = _iota2((c_len, c_len), 1)
    tri_incl = (row64 >= col64).astype(F32)
    causal = row64 >= col64
    strict = row64 > col64
    g_all = -jnp.exp(alog_ref[...]) * _softplus(ba + dtb_ref[...])
    gc_all = _dot_f32(tri_incl, g_all)

    grp = list(range(n_grp))
    sls = [slice(gi * d, (gi + 1) * d) for gi in grp]

    def l2n(x):
        return x * lax.rsqrt(jnp.sum(x * x, axis=1, keepdims=True) + 1e-6)

    def pick(mat, lane_idx):
        return jnp.sum(jnp.where(lane == lane_idx, mat, 0.0), axis=1, keepdims=True)

    def decay_of(gc):
        gc_col = jnp.broadcast_to(gc, (c_len, c_len))
        gc_row = jnp.sum(jnp.where(row64 == col64, gc_col, 0.0), axis=0, keepdims=True)
        return jnp.exp(jnp.where(causal, gc_col - gc_row, NEG_BIG))

    q = [l2n(q_all[:, sl]) * (d ** -0.5) for sl in sls]
    k = [l2n(k_all[:, sl]) for sl in sls]
    v = [v_all[:, sl] for sl in sls]
    beta = [_sigmoid(pick(ba, hg * n_grp + gi)) for gi in grp]
    gc = [pick(gc_all, nh + hg * n_grp + gi) for gi in grp]
    gc_last = [x[c_len - 1:c_len, :] for x in gc]
    decay = _each(decay_of, gc)
    eg = _each(jnp.exp, gc)
    kb = _each(lambda a, b: a * b, k, beta)
    kkt = _each(_dot_nt, kb, k)
    low = _each(lambda m, dc: jnp.where(strict, m * dc, 0.0), kkt, decay)
    t_inv = _unit_lower_inverse(low, c_len, c_len)
    u = _each(lambda t, a, b: _dot(t, a * b), t_inv, v, beta)
    w = _each(lambda t, a, e: _dot(t, a * e), t_inv, kb, eg)
    qk = _each(lambda a, b, dc: _dot_nt(a, b) * dc, q, k, decay)
    s = [s_ref[gi] for gi in grp]
    ws = _each(_dot, w, s)
    v_new = _each(lambda a, b: a - b, u, ws)
    qs = _each(lambda a, e, st: _dot(a * e, st), q, eg, s)
    intra = _each(_dot, qk, v_new)
    kv = _each(lambda a, gl, g1, vn: _dot_tn(a * jnp.exp(gl - g1), vn), k, gc_last, gc, v_new)
    for gi in grp:
        s_ref[gi] = s[gi] * jnp.exp(gc_last[gi]) + kv[gi]
        out = qs[gi] + intra[gi]
        out = out * lax.rsqrt(jnp.mean(out * out, axis=1, keepdims=True) + DN_NORM_EPS)
        o_ref[:, sls[gi]] = (out * gate_all[:, sls[gi]]).astype(o_ref.dtype)


def _gated_deltanet(proj, tail, ba_blk, conv_w, alog_pad, dtb_pad, norm_w, bsz, seq, n_heads, n_grp):
    t_tot = bsz * seq
    nc = seq // CHUNK
    d = DN_HEAD_DIM
    assert n_heads % n_grp == 0
    n_hg = n_heads // n_grp
    gw = n_grp * d

    def col(group):
        return pl.BlockSpec((CHUNK, gw), lambda b, h, c, g=group: (b * nc + c, g * n_hg + h))

    def cw(group):
        return pl.BlockSpec((DN_CONV, gw), lambda b, h, c, g=group: (0, g * n_hg + h))

    vec = pl.BlockSpec((1, LANES), lambda b, h, c: (0, 0))
    return pl.pallas_call(
        _dn_kernel, grid=(bsz, n_hg, nc),
        in_specs=[col(0), col(1), col(2), col(3),
                  pl.BlockSpec((CHUNK, LANES), lambda b, h, c: (b * nc + c, ba_blk)),
                  cw(0), cw(1), cw(2), vec, vec,
                  pl.BlockSpec((1, gw), lambda b, h, c: (0, 0))],
        out_specs=pl.BlockSpec((CHUNK, gw), lambda b, h, c: (b * nc + c, h)),
        out_shape=jax.ShapeDtypeStruct((t_tot, n_heads * d), BF16),
        scratch_shapes=[pltpu.VMEM((CHUNK + SUBLANES, gw), F32)] * 3 + [pltpu.VMEM((n_grp, d, d), F32)],
        compiler_params=_cparams(("arbitrary", "arbitrary", "arbitrary")), name="gated_deltanet",
    )(proj, proj, proj, proj, tail, conv_w, conv_w, conv_w, alog_pad, dtb_pad, jnp.tile(norm_w, (1, n_grp)))


def _rw_kernel(has_vres, *refs):
    if has_vres:
        (r_ref, k_ref, v_ref, lo_ref, vf_ref, mur_ref, muk_ref, muv_ref, mulo_ref, w0_ref, a0_ref,
         kk_ref, ka_ref, rk_ref, gw_ref, gb_ref, w2_ref, a2_ref, g2_ref, v0_ref, v2_ref,
         o_ref, rbuf, kbuf, vbuf, lobuf, ht_ref) = refs
    else:
        (r_ref, k_ref, v_ref, lo_ref, mur_ref, muk_ref, muv_ref, mulo_ref, w0_ref, a0_ref,
         kk_ref, ka_ref, rk_ref, gw_ref, gb_ref, w2_ref, a2_ref, g2_ref,
         o_ref, vf_out_ref, rbuf, kbuf, vbuf, lobuf, ht_ref) = refs
    c_len, n = CHUNK, RW_HEAD_DIM
    n_grp = ht_ref.shape[0]
    c = pl.program_id(2)
    halo = SUBLANES
    bufs = (rbuf, kbuf, vbuf, lobuf)

    @pl.when(c == 0)
    def _():
        ht_ref[...] = jnp.zeros_like(ht_ref)
        for buf in bufs:
            buf[0:halo, :] = jnp.zeros((halo, buf.shape[1]), F32)

    @pl.when(c > 0)
    def _():
        for buf in bufs:
            buf[0:halo, :] = buf[c_len:c_len + halo, :]

    def shift_mix(buf, x_ref, mu_ref):
        x = x_ref[...]
        buf[halo:halo + c_len, :] = x
        prev = buf[halo - 1:halo - 1 + c_len, :]
        return x + (prev - x) * mu_ref[...]

    r_all = shift_mix(rbuf, r_ref, mur_ref)
    k_all = shift_mix(kbuf, k_ref, muk_ref)
    v_all = shift_mix(vbuf, v_ref, muv_ref)
    lo = shift_mix(lobuf, lo_ref, mulo_ref)
    w_pre = w0_ref[...] + _dot(jnp.tanh(lo[:, 0:128]), w2_ref[...])
    w_log = -_softplus(-w_pre) - 0.5
    logd_all = -jnp.exp(w_log)
    a_all = _sigmoid(a0_ref[...] + _dot(lo[:, 128:256], a2_ref[...]))
    g_all = _dot(_sigmoid(lo[:, 256:512]), g2_ref[...])
    if has_vres:
        v_all = v_all + (vf_ref[...] - v_all) * _sigmoid(v0_ref[...] + _dot(lo[:, 512:640], v2_ref[...]))
    else:
        vf_out_ref[...] = v_all
    kk_all = k_all * kk_ref[...]
    k_all = k_all * (1.0 + (a_all - 1.0) * ka_ref[...])
    rk_all = r_all * k_all * rk_ref[...]

    row64 = _iota2((c_len, c_len), 0)
    col64 = _iota2((c_len, c_len), 1)
    tri_incl = (row64 >= col64).astype(F32)
    gcum_all = _dot_f32(tri_incl, logd_all)
    p_incl_all = jnp.exp(gcum_all)
    p_prev_all = jnp.exp(gcum_all - logd_all)
    p_inv_all = jnp.exp(-gcum_all)

    r128 = _iota2((LANES, LANES), 0)
    c128 = _iota2((LANES, LANES), 1)
    same_head = (r128 >> 6) == (c128 >> 6)
    seg_ones = same_head.astype(F32)
    strict = r128 > c128
    incl = r128 >= c128
    head0 = _iota2((c_len, LANES), 1) < n

    def seg_sum(x):
        return _dot_f32(x, seg_ones)

    def stack(x):
        return jnp.concatenate([jnp.where(head0, x, 0.0), jnp.where(head0, 0.0, x)], axis=0)

    grp = list(range(n_grp))
    sls = [slice(gi * LANES, (gi + 1) * LANES) for gi in grp]

    def cut(x):
        return [x[:, sl] for sl in sls]

    r, k, v, a = cut(r_all), cut(k_all), cut(v_all), cut(a_all)
    p_incl, p_prev, p_inv = cut(p_incl_all), cut(p_prev_all), cut(p_inv_all)
    kk = cut(kk_all)
    kk_ss = _each(lambda x: seg_sum(x * x), kk)
    kk = _each(lambda x, ss: x * lax.rsqrt(ss + 1e-12), kk, kk_ss)
    a_s = _each(lambda x, p: stack(-x * p), kk, p_prev)
    r_s = _each(lambda x, p: stack(x * p), r, p_incl)
    b_s = _each(lambda x, y, p: stack(x * y * p), kk, a, p_inv)
    k_s = _each(lambda x, p: stack(x * p), k, p_inv)
    v_s = _each(stack, v)

    a_ab = _each(_dot_nt, a_s, b_s)
    a_ak = _each(_dot_nt, a_s, k_s)
    a_rb = _each(_dot_nt, r_s, b_s)
    a_rk = _each(_dot_nt, r_s, k_s)
    t_inv = _unit_lower_inverse(_each(lambda m: jnp.where(strict, -m, 0.0), a_ab), 2 * c_len, c_len)
    akv = _each(lambda m, x: _dot(jnp.where(strict, m, 0.0), x), a_ak, v_s)
    u0 = _each(_dot, t_inv, akv)
    w = _each(_dot, t_inv, a_s)

    ht = [ht_ref[gi] for gi in grp]
    wh = _each(_dot_nt, w, ht)
    u = _each(lambda x, y: x + y, wh, u0)
    y_h = _each(_dot_nt, r_s, ht)
    y_u = _each(lambda m, x: _dot(jnp.where(incl, m, 0.0), x), a_rb, u)
    y_v = _each(lambda m, x: _dot(jnp.where(incl, m, 0.0), x), a_rk, v_s)
    h_u = _each(_dot_tn, u, b_s)
    h_v = _each(_dot_tn, v_s, k_s)
    y = []
    for gi in grp:
        p_last = p_incl[gi][c_len - 1:c_len, :]
        ht_ref[gi] = jnp.where(same_head, (ht[gi] + h_u[gi] + h_v[gi]) * p_last, 0.0)
        y_s = y_h[gi] + y_u[gi] + y_v[gi]
        y.append(y_s[0:c_len, :] + y_s[c_len:2 * c_len, :])

    mu = _each(lambda x: seg_sum(x) * (1.0 / n), y)
    yc = _each(lambda x, m: x - m, y, mu)
    var = _each(lambda x: seg_sum(x * x) * (1.0 / n), yc)
    bonus = _each(seg_sum, cut(rk_all))
    for gi in grp:
        sl = sls[gi]
        yn = yc[gi] * lax.rsqrt(var[gi] + RW_GN_EPS) * gw_ref[:, sl] + gb_ref[:, sl]
        o_ref[:, sl] = ((yn + bonus[gi] * v[gi]) * g_all[:, sl]).astype(o_ref.dtype)


def _rwkv7(proj, tail, r_blk0, v_first, prm, bsz, seq, n_pairs, n_grp):
    t_tot = bsz * seq
    nc = seq // CHUNK
    w_rw = n_pairs * LANES
    gw = n_grp * LANES
    assert n_pairs % n_grp == 0 and r_blk0 % n_grp == 0
    n_pg = n_pairs // n_grp
    has_vres = v_first is not None

    def col(group):
        return pl.BlockSpec((CHUNK, gw), lambda b, p, c, g=group: (b * nc + c, r_blk0 // n_grp + g * n_pg + p))

    tok = pl.BlockSpec((CHUNK, gw), lambda b, p, c: (b * nc + c, p))
    lo_spec = pl.BlockSpec((CHUNK, 5 * LANES), lambda b, p, c: (b * nc + c, 0))
    vec = pl.BlockSpec((1, gw), lambda b, p, c: (0, p))
    vec_lo = pl.BlockSpec((1, 5 * LANES), lambda b, p, c: (0, 0))

    def lowrank(rank):
        return pl.BlockSpec((rank, gw), lambda b, p, c: (0, p))

    in_specs = [col(0), col(1), col(2), lo_spec]
    args = [proj, proj, proj, tail]
    if has_vres:
        in_specs.append(tok)
        args.append(v_first)
    in_specs += [vec, vec, vec, vec_lo] + [vec] * 7 + [lowrank(128), lowrank(128), lowrank(256)]
    args += [prm["mu_r"], prm["mu_k"], prm["mu_v"], prm["mu_lo"], prm["w0"], prm["a0"], prm["k_k"],
             prm["k_a"], prm["r_k"], prm["gn_w"], prm["gn_b"], prm["w2"], prm["a2"], prm["g2"]]
    if has_vres:
        in_specs += [vec, lowrank(128)]
        args += [prm["v0"], prm["v2"]]
    out_shape = [jax.ShapeDtypeStruct((t_tot, w_rw), BF16)]
    out_specs = [tok]
    if not has_vres:
        out_shape.append(jax.ShapeDtypeStruct((t_tot, w_rw), F32))
        out_specs.append(tok)
    halo_rows = CHUNK + SUBLANES
    outs = pl.pallas_call(
        functools.partial(_rw_kernel, has_vres), grid=(bsz, n_pg, nc),
        in_specs=in_specs, out_specs=out_specs, out_shape=out_shape,
        scratch_shapes=[pltpu.VMEM((halo_rows, gw), F32)] * 3
        + [pltpu.VMEM((halo_rows, 5 * LANES), F32), pltpu.VMEM((n_grp, LANES, LANES), F32)],
        compiler_params=_cparams(("arbitrary", "arbitrary", "arbitrary")), name="rwkv7",
    )(*args)
    if has_vres:
        return outs[0], v_first
    return outs[0], outs[1]


def _layer_norm_rows(y, g, b):
    mu = jnp.mean(y, axis=1, keepdims=True)
    yc = y - mu
    var = jnp.mean(yc * yc, axis=1, keepdims=True)
    return yc * lax.rsqrt(var + LN_EPS) * g + b


def _ln_router_kernel(y_ref, g_ref, b_ref, rw_ref, rb_ref, x_ref, ti_ref, tw_ref, pos_ref, cnt_ref,
                      base_ref):
    tm = y_ref.shape[0]
    i = pl.program_id(0)

    @pl.when(i == 0)
    def _():
        base_ref[...] = jnp.zeros_like(base_ref)

    x = _layer_norm_rows(y_ref[...], g_ref[...], b_ref[...])
    x_ref[...] = x
    logits = _dot(x, rw_ref[...]) + rb_ref[...]
    lane = _iota2((tm, LANES), 1)
    lane_f = lane.astype(F32)
    work = logits
    sels, vals, idxs = [], [], []
    for _ in range(TOP_K):
        m = jnp.max(work, axis=1, keepdims=True)
        idx = jnp.min(jnp.where(work == m, lane_f, float(LANES)), axis=1, keepdims=True)
        sel = lane_f == idx
        sels.append(sel)
        vals.append(m)
        idxs.append(idx)
        work = jnp.where(sel, 2.0 * NEG_BIG, work)
    exps = [jnp.exp(m - vals[0]) for m in vals]
    denom = exps[0] + exps[1] + exps[2] + exps[3]
    cnt = jnp.zeros((tm, LANES), F32)
    for sel in sels:
        cnt = cnt + sel.astype(F32)
    rowt = _iota2((tm, tm), 0)
    colt = _iota2((tm, tm), 1)
    before = _dot((rowt > colt).astype(F32), cnt)
    posmat = base_ref[...] + before
    ti = jnp.zeros((tm, LANES), F32)
    tw = jnp.zeros((tm, LANES), F32)
    pos = jnp.zeros((tm, LANES), F32)
    for kk in range(TOP_K):
        here = lane == kk
        ti = jnp.where(here, idxs[kk], ti)
        tw = jnp.where(here, exps[kk] / denom, tw)
        pos = jnp.where(here, jnp.sum(jnp.where(sels[kk], posmat, 0.0), axis=1, keepdims=True), pos)
    ti_ref[...] = ti.astype(I32)
    tw_ref[...] = tw
    pos_ref[...] = pos.astype(I32)
    base_new = base_ref[...] + jnp.sum(cnt, axis=0, keepdims=True)
    base_ref[...] = base_new
    cnt_ref[...] = jnp.broadcast_to(base_new, cnt_ref.shape).astype(I32)


def _ln_router(y, g, b, rw_pad, rb_pad, tm):
    t_tot, dm = y.shape
    tm = min(tm, t_tot)
    assert t_tot % tm == 0
    row = pl.BlockSpec((tm, dm), lambda i: (i, 0))
    vec = pl.BlockSpec((1, dm), lambda i: (0, 0))
    small = pl.BlockSpec((tm, LANES), lambda i: (i, 0))
    return pl.pallas_call(
        _ln_router_kernel, grid=(t_tot // tm,),
        in_specs=[row, vec, vec, pl.BlockSpec((dm, LANES), lambda i: (0, 0)),
                  pl.BlockSpec((1, LANES), lambda i: (0, 0))],
        out_specs=[row, small, small, small, pl.BlockSpec((SUBLANES, LANES), lambda i: (0, 0))],
        out_shape=[jax.ShapeDtypeStruct((t_tot, dm), F32),
                   jax.ShapeDtypeStruct((t_tot, LANES), I32),
                   jax.ShapeDtypeStruct((t_tot, LANES), F32),
                   jax.ShapeDtypeStruct((t_tot, LANES), I32),
                   jax.ShapeDtypeStruct((SUBLANES, LANES), I32)],
        scratch_shapes=[pltpu.VMEM((1, LANES), F32)],
        compiler_params=_cparams(("arbitrary",)), name="ln_router",
    )(y, g, b, rw_pad, rb_pad)


def _row_gather_start(src_hbm, dst_buf, sem, idx_of_row, n_rows):
    def body(r, carry):
        pltpu.make_async_copy(src_hbm.at[pl.ds(idx_of_row(r), 1)], dst_buf.at[pl.ds(r, 1)], sem).start()
        return carry
    lax.fori_loop(0, n_rows, body, 0, unroll=8)


def _row_gather_wait(src_hbm, dst_buf, sem, n_rows):
    pltpu.make_async_copy(src_hbm.at[pl.ds(0, n_rows)], dst_buf, sem).wait()


def _moe_up_kernel(te_ref, nu_ref, tos_ref, x_hbm, wg_ref, wu_ref, bg_ref, bu_ref, h_ref, xbuf, sem):
    tm = h_ref.shape[0]
    i = pl.program_id(0)
    slot = lax.rem(i, 2)
    n_used = nu_ref[0]

    def start(tile, s):
        _row_gather_start(x_hbm, xbuf.at[s], sem.at[s], lambda r: tos_ref[tile * tm + r], tm)

    @pl.when(i == 0)
    def _():
        start(0, 0)

    @pl.when(i + 1 < n_used)
    def _():
        start(i + 1, 1 - slot)

    @pl.when(i < n_used)
    def _():
        _row_gather_wait(x_hbm, xbuf.at[slot], sem.at[slot], tm)
        x = xbuf[slot].astype(BF16)
        gate = jnp.dot(x, wg_ref[...], preferred_element_type=F32) + bg_ref[...]
        up = jnp.dot(x, wu_ref[...], preferred_element_type=F32) + bu_ref[...]
        gate = jnp.minimum(gate, SWIGLU_LIMIT)
        up = jnp.clip(up, -SWIGLU_LIMIT, SWIGLU_LIMIT)
        h_ref[...] = ((up + 1.0) * gate * _sigmoid(gate * SWIGLU_ALPHA)).astype(h_ref.dtype)

    @pl.when(i >= n_used)
    def _():
        h_ref[...] = jnp.zeros_like(h_ref)


def _moe_down_kernel(te_ref, nu_ref, h_ref, wd_ref, bd_ref, y_ref):
    i = pl.program_id(0)

    @pl.when(i < nu_ref[0])
    def _():
        y_ref[...] = (jnp.dot(h_ref[...], wd_ref[...], preferred_element_type=F32) + bd_ref[...]).astype(y_ref.dtype)

    @pl.when(i >= nu_ref[0])
    def _():
        y_ref[...] = jnp.zeros_like(y_ref)


def _moe_experts(x, tile_expert, n_used, tok_of_slot, wg, wu, bg, bu, wd, bd, tm):
    n_slots = tok_of_slot.shape[0]
    n_tiles = n_slots // tm
    n_exp, dm, de = wg.shape
    h = pl.pallas_call(
        _moe_up_kernel,
        grid_spec=pltpu.PrefetchScalarGridSpec(
            num_scalar_prefetch=3, grid=(n_tiles,),
            in_specs=[pl.BlockSpec(memory_space=pl.ANY),
                      pl.BlockSpec((None, dm, de), lambda i, te, nu, tos: (te[i], 0, 0)),
                      pl.BlockSpec((None, dm, de), lambda i, te, nu, tos: (te[i], 0, 0)),
                      pl.BlockSpec((None, 1, de), lambda i, te, nu, tos: (te[i], 0, 0)),
                      pl.BlockSpec((None, 1, de), lambda i, te, nu, tos: (te[i], 0, 0))],
            out_specs=pl.BlockSpec((tm, de), lambda i, te, nu, tos: (i, 0)),
            scratch_shapes=[pltpu.VMEM((2, tm, dm), F32), pltpu.SemaphoreType.DMA((2,))]),
        out_shape=jax.ShapeDtypeStruct((n_slots, de), BF16),
        compiler_params=_cparams(("arbitrary",)), name="moe_gate_up",
    )(tile_expert, n_used, tok_of_slot, x, wg, wu, bg, bu)
    return pl.pallas_call(
        _moe_down_kernel,
        grid_spec=pltpu.PrefetchScalarGridSpec(
            num_scalar_prefetch=2, grid=(n_tiles,),
            in_specs=[pl.BlockSpec((tm, de), lambda i, te, nu: (i, 0)),
                      pl.BlockSpec((None, de, dm), lambda i, te, nu: (te[i], 0, 0)),
                      pl.BlockSpec((None, 1, dm), lambda i, te, nu: (te[i], 0, 0))],
            out_specs=pl.BlockSpec((tm, dm), lambda i, te, nu: (i, 0))),
        out_shape=jax.ShapeDtypeStruct((n_slots, dm), F32),
        compiler_params=_cparams(("arbitrary",)), name="moe_down",
    )(tile_expert, n_used, h, wd, bd)


def _combine_ln_kernel(alpha, dest_ref, ys_hbm, x_ref, tw_ref, g_ref, b_ref, o_ref, ob_ref, ybuf, sem):
    tm = x_ref.shape[0]
    i = pl.program_id(0)
    n = pl.num_programs(0)
    slot = lax.rem(i, 2)

    def start(tile, s):
        for kk in range(TOP_K):
            _row_gather_start(ys_hbm, ybuf.at[s, kk], sem.at[s],
                              lambda r, kk=kk: dest_ref[(tile * tm + r) * TOP_K + kk], tm)

    @pl.when(i == 0)
    def _():
        start(0, 0)

    @pl.when(i + 1 < n)
    def _():
        start(i + 1, 1 - slot)

    for kk in range(TOP_K):
        _row_gather_wait(ys_hbm, ybuf.at[slot, kk], sem.at[slot], tm)
    tw = tw_ref[...]
    acc = alpha * x_ref[...]
    for kk in range(TOP_K):
        acc = acc + tw[:, kk:kk + 1] * ybuf[slot, kk]
    out = _layer_norm_rows(acc, g_ref[...], b_ref[...])
    o_ref[...] = out
    ob_ref[...] = out.astype(BF16)


def _combine_ln(ys, x, tw, dest_flat, g, b, alpha, tm):
    t_tot, dm = x.shape
    tm = min(tm, t_tot)
    assert t_tot % tm == 0
    row = pl.BlockSpec((tm, dm), lambda i, d: (i, 0))
    vec = pl.BlockSpec((1, dm), lambda i, d: (0, 0))
    return pl.pallas_call(
        functools.partial(_combine_ln_kernel, alpha),
        grid_spec=pltpu.PrefetchScalarGridSpec(
            num_scalar_prefetch=1, grid=(t_tot // tm,),
            in_specs=[pl.BlockSpec(memory_space=pl.ANY), row,
                      pl.BlockSpec((tm, LANES), lambda i, d: (i, 0)), vec, vec],
            out_specs=[row, row],
            scratch_shapes=[pltpu.VMEM((2, TOP_K, tm, dm), F32), pltpu.SemaphoreType.DMA((2,))]),
        out_shape=[jax.ShapeDtypeStruct((t_tot, dm), F32), jax.ShapeDtypeStruct((t_tot, dm), BF16)],
        compiler_params=_cparams(("arbitrary",)), name="moe_combine_ln",
    )(dest_flat, ys, x, tw, g, b)


def _tiles(t_tot):
    return dict(mm_m=min(1024, t_tot), mm_n=1024, ln_m=min(256, t_tot), moe_m=min(512, t_tot),
                comb_m=min(64, t_tot), rec_grp=16)


def _pad_cols(w, width):
    return jnp.pad(w, ((0, 0), (0, width - w.shape[1])))


def _pad_rows(w, rows):
    return jnp.pad(w, ((0, rows - w.shape[0]), (0, 0)))


def kernel(x, mix_w_in, mix_w_in_vres, dn_conv_w, dn_a_log, dn_dt_bias, dn_norm_w, rw_shift_mu, rw_shift_mu_vres, rw_w0, rw_w2, rw_a0, rw_a2, rw_g2, rw_v0, rw_v2, rw_k_k, rw_k_a, rw_r_k, rw_gn_w, rw_gn_b, mix_w_out, ln1_g, ln1_b, router_w, router_b, exp_w_gate, exp_b_gate, exp_w_up, exp_b_up, exp_w_down, exp_b_down, ln2_g, ln2_b):
    bsz, seq, dm = x.shape
    depth = mix_w_in.shape[0]
    t_tot = bsz * seq
    dn_heads = dn_a_log.shape[1]
    w_dn = dn_heads * DN_HEAD_DIM
    w_rw = rw_w0.shape[1]
    n_pairs = w_rw // LANES
    r_dec, r_aaa, r_gate, r_mv = rw_w2.shape[1], rw_a2.shape[1], rw_g2.shape[1], rw_v2.shape[1]
    n_exp = router_w.shape[2]
    assert seq % CHUNK == 0 and w_rw % LANES == 0 and 2 * dn_heads <= LANES
    assert max(r_dec, r_aaa, r_mv) <= LANES and r_gate == 2 * LANES and n_exp <= LANES
    alpha = (2 * depth) ** 0.25
    tl = _tiles(t_tot)
    n_dn_in = 4 * w_dn + 2 * dn_heads

    x2d = x.reshape(t_tot, dm)
    xb = x2d.astype(BF16)
    v_first = None
    for l in range(depth):
        w_in = mix_w_in[l]
        rw0 = n_dn_in
        o_lo = rw0 + 3 * w_rw
        w_main = jnp.concatenate([w_in[:, :4 * w_dn], w_in[:, rw0:o_lo]], axis=1).astype(BF16)
        v_lo_w = mix_w_in_vres[l - 1] if l > 0 else jnp.zeros((dm, r_mv), F32)
        w_tail = jnp.concatenate([
            _pad_cols(w_in[:, o_lo:o_lo + r_dec], LANES),
            _pad_cols(w_in[:, o_lo + r_dec:o_lo + r_dec + r_aaa], LANES),
            w_in[:, o_lo + r_dec + r_aaa:o_lo + r_dec + r_aaa + r_gate],
            _pad_cols(v_lo_w, LANES),
            _pad_cols(w_in[:, 4 * w_dn:n_dn_in], LANES)], axis=1).astype(BF16)
        proj = _matmul(xb, w_main, tl["mm_m"], tl["mm_n"], F32, name="in_proj")
        tail = _matmul(xb, w_tail, tl["mm_m"], w_tail.shape[1], F32, name="in_proj_tail")

        zeros_h = jnp.zeros((dn_heads,), F32)
        alog_pad = _pad_cols(jnp.concatenate([zeros_h, dn_a_log[l]])[None, :], LANES)
        dtb_pad = _pad_cols(jnp.concatenate([zeros_h, dn_dt_bias[l]])[None, :], LANES)
        o_dn = _gated_deltanet(proj, tail, 5, dn_conv_w[l], alog_pad, dtb_pad, dn_norm_w[l][None, :],
                               bsz, seq, dn_heads, min(tl["rec_grp"], dn_heads))

        mu = rw_shift_mu[l]
        mu_lo_parts = [_pad_cols(mu[None, 3 * w_rw:3 * w_rw + r_dec], LANES),
                       _pad_cols(mu[None, 3 * w_rw + r_dec:3 * w_rw + r_dec + r_aaa], LANES),
                       mu[None, 3 * w_rw + r_dec + r_aaa:],
                       _pad_cols(rw_shift_mu_vres[l - 1][None, :] if l > 0 else jnp.zeros((1, r_mv), F32), LANES)]
        prm = dict(mu_r=mu[None, :w_rw], mu_k=mu[None, w_rw:2 * w_rw], mu_v=mu[None, 2 * w_rw:3 * w_rw],
                   mu_lo=jnp.concatenate(mu_lo_parts, axis=1),
                   w0=rw_w0[l][None, :], a0=rw_a0[l][None, :], k_k=rw_k_k[l][None, :], k_a=rw_k_a[l][None, :],
                   r_k=rw_r_k[l].reshape(1, w_rw), gn_w=rw_gn_w[l][None, :], gn_b=rw_gn_b[l][None, :],
                   w2=_pad_rows(rw_w2[l], LANES), a2=_pad_rows(rw_a2[l], LANES), g2=rw_g2[l])
        if l > 0:
            prm["v0"] = rw_v0[l - 1][None, :]
            prm["v2"] = _pad_rows(rw_v2[l - 1], LANES)
        o_rw, v_first = _rwkv7(proj, tail, 4 * dn_heads, v_first if l > 0 else None, prm, bsz, seq, n_pairs,
                               min(tl["rec_grp"], n_pairs))

        mix = jnp.concatenate([o_dn, o_rw], axis=1)
        y1 = _matmul(mix, mix_w_out[l].astype(BF16), tl["mm_m"], tl["mm_n"], F32, res=x2d, alpha=alpha,
                     name="out_proj")
        rw_pad = _pad_cols(router_w[l], LANES)
        rb_pad = jnp.concatenate([router_b[l], jnp.full((LANES - n_exp,), NEG_BIG, F32)])[None, :]
        x1, top_i, top_w, pos, cnt = _ln_router(y1, ln1_g[l][None, :], ln1_b[l][None, :], rw_pad, rb_pad,
                                                tl["ln_m"])

        tm = tl["moe_m"]
        counts = cnt[0, :n_exp]
        padded = ((counts + tm - 1) // tm) * tm
        ends = jnp.cumsum(padded)
        offs = ends - padded
        n_tiles = (t_tot * TOP_K) // tm + n_exp
        dest = offs[top_i[:, :TOP_K]] + pos[:, :TOP_K]
        tok_ids = jnp.broadcast_to(jnp.arange(t_tot, dtype=I32)[:, None], (t_tot, TOP_K))
        tok_of_slot = jnp.zeros((n_tiles * tm,), I32).at[dest.reshape(-1)].set(tok_ids.reshape(-1))
        tile_start = jnp.arange(n_tiles, dtype=I32) * tm
        tile_expert = jnp.minimum(jnp.searchsorted(ends, tile_start, side="right"), n_exp - 1).astype(I32)
        n_used = (ends[-1] // tm).astype(I32)[None]

        ys = _moe_experts(x1, tile_expert, n_used, tok_of_slot,
                          exp_w_gate[l].astype(BF16), exp_w_up[l].astype(BF16),
                          exp_b_gate[l][:, None, :], exp_b_up[l][:, None, :],
                          exp_w_down[l].astype(BF16), exp_b_down[l][:, None, :], tm)
        x2d, xb = _combine_ln(ys, x1, top_w, dest.reshape(-1), ln2_g[l][None, :], ln2_b[l][None, :],
                              alpha, tl["comb_m"])
    return x2d.reshape(bsz, seq, dm)
```

```python
import functools
import math

import jax
import jax.numpy as jnp
from jax import lax
from jax.experimental import pallas as pl
from jax.experimental.pallas import tpu as pltpu

F32 = jnp.float32
BF16 = jnp.bfloat16
I32 = jnp.int32

LANES = 128
SUBLANES = 8
VMEM_LIMIT = 56 * 1024 * 1024

CHUNK = 64
DN_HEAD_DIM = 128
DN_CONV = 4
DN_NORM_EPS = 1e-6
RW_HEAD_DIM = 64
RW_GN_EPS = 64e-5
TOP_K = 4
SWIGLU_ALPHA = 1.702
SWIGLU_LIMIT = 7.0
LN_EPS = 1e-5
NEG_BIG = -1e30


def _dot(a, b):
    return jnp.dot(a.astype(BF16), b.astype(BF16), preferred_element_type=F32)


def _dot_nt(a, b):
    return lax.dot_general(a.astype(BF16), b.astype(BF16), (((1,), (1,)), ((), ())),
                           preferred_element_type=F32)


def _dot_tn(a, b):
    return lax.dot_general(a.astype(BF16), b.astype(BF16), (((0,), (0,)), ((), ())),
                           preferred_element_type=F32)


def _dot_split(a, ones, passes, ones_left=False):
    acc, rem = None, a
    for p in range(passes):
        term = rem.astype(BF16)
        part = (jnp.dot(ones, term, preferred_element_type=F32) if ones_left
                else jnp.dot(term, ones, preferred_element_type=F32))
        acc = part if acc is None else acc + part
        if p + 1 < passes:
            rem = rem - term.astype(F32)
    return acc


def _iota2(shape, dim):
    return lax.broadcasted_iota(I32, shape, dim)


def _sigmoid(x):
    return 1.0 / (1.0 + jnp.exp(-x))


def _softplus(x):
    return jnp.maximum(x, 0.0) + jnp.log(1.0 + jnp.exp(-jnp.abs(x)))


def _silu(x):
    return x * _sigmoid(x)


def _each(fn, *lists):
    return [fn(*args) for args in zip(*lists)]


def _unit_lower_inverse(lows, n, blk):
    row = _iota2((n, n), 0)
    col = _iota2((n, n), 1)
    eye = (row == col).astype(F32)
    pair = (row >> 1) == (col >> 1)
    invs = _each(lambda low: eye - jnp.where(pair, low, 0.0), lows)
    shift = 1
    while (1 << shift) < blk:
        same_big = (row >> (shift + 1)) == (col >> (shift + 1))
        diff_small = (row >> shift) != (col >> shift)
        mask = jnp.logical_and(same_big, diff_small)
        cms = _each(lambda low: jnp.where(mask, low, 0.0), lows)
        tmp = _each(_dot, invs, cms)
        upd = _each(_dot, tmp, invs)
        invs = _each(lambda inv, d: inv - d, invs, upd)
        shift += 1
    return invs


def _cparams(sem):
    return pltpu.CompilerParams(dimension_semantics=sem, vmem_limit_bytes=VMEM_LIMIT)


def _mm_kernel(x_ref, w_ref, o_ref):
    o_ref[...] = jnp.dot(x_ref[...], w_ref[...], preferred_element_type=F32).astype(o_ref.dtype)


def _mm_res_kernel(alpha, x_ref, w_ref, r_ref, o_ref):
    acc = jnp.dot(x_ref[...], w_ref[...], preferred_element_type=F32)
    o_ref[...] = (alpha * r_ref[...] + acc).astype(o_ref.dtype)


def _matmul(x, w, tm, tn, out_dtype, res=None, alpha=1.0, name="matmul"):
    m, k = x.shape
    n = w.shape[1]
    tm = min(tm, m)
    tn = max(t for t in range(LANES, min(tn, n) + 1, LANES) if n % t == 0)
    assert m % tm == 0 and n % LANES == 0
    in_specs = [pl.BlockSpec((tm, k), lambda i, j: (i, 0)),
                pl.BlockSpec((k, tn), lambda i, j: (0, j))]
    args = [x, w]
    body = _mm_kernel
    if res is not None:
        in_specs.append(pl.BlockSpec((tm, tn), lambda i, j: (i, j)))
        args.append(res)
        body = functools.partial(_mm_res_kernel, alpha)
    return pl.pallas_call(
        body, grid=(m // tm, n // tn), in_specs=in_specs,
        out_specs=pl.BlockSpec((tm, tn), lambda i, j: (i, j)),
        out_shape=jax.ShapeDtypeStruct((m, n), out_dtype),
        compiler_params=_cparams(("arbitrary", "arbitrary")), name=name)(*args)


def _mm_wcast_kernel(n_lhs, alpha, has_res, *refs):
    lhs = refs[:n_lhs]
    w_ref = refs[n_lhs]
    r_ref = refs[n_lhs + 1] if has_res else None
    o_ref = refs[n_lhs + 1 + int(has_res)]
    wbf = refs[-1]

    @pl.when(pl.program_id(1) == 0)
    def _():
        wbf[...] = w_ref[...].astype(BF16)

    acc, off = None, 0
    for lr in lhs:
        kk = lr.shape[1]
        part = jnp.dot(lr[...], wbf[off:off + kk, :], preferred_element_type=F32)
        acc = part if acc is None else acc + part
        off += kk
    if has_res:
        acc = alpha * r_ref[...] + acc
    o_ref[...] = acc.astype(o_ref.dtype)


def _matmul_wcast(lhs_list, w, layer, n_cols, tm, tn, out_dtype, res=None, alpha=1.0, name="matmul"):
    m = lhs_list[0].shape[0]
    k = w.shape[-2]
    assert sum(a.shape[1] for a in lhs_list) == k
    tm = min(tm, m)
    tn = max(t for t in range(LANES, min(tn, n_cols) + 1, LANES) if n_cols % t == 0)
    assert m % tm == 0
    in_specs = [pl.BlockSpec((tm, a.shape[1]), lambda j, i: (i, 0)) for a in lhs_list]
    if w.ndim == 3:
        in_specs.append(pl.BlockSpec((None, k, tn), lambda j, i: (layer, 0, j)))
    else:
        in_specs.append(pl.BlockSpec((k, tn), lambda j, i: (0, j)))
    args = list(lhs_list) + [w]
    if res is not None:
        in_specs.append(pl.BlockSpec((tm, tn), lambda j, i: (i, j)))
        args.append(res)
    return pl.pallas_call(
        functools.partial(_mm_wcast_kernel, len(lhs_list), alpha, res is not None),
        grid=(n_cols // tn, m // tm), in_specs=in_specs,
        out_specs=pl.BlockSpec((tm, tn), lambda j, i: (i, j)),
        out_shape=jax.ShapeDtypeStruct((m, n_cols), out_dtype),
        scratch_shapes=[pltpu.VMEM((k, tn), BF16)],
        compiler_params=_cparams(("arbitrary", "arbitrary")), name=name)(*args)


def _dn_kernel(q_ref, k_ref, v_ref, z_ref, ba_ref, cwq_ref, cwk_ref, cwv_ref, alog_ref, dtb_ref,
               nw_ref, o_ref, qbuf, kbuf, vbuf, s_ref):
    c_len, d = CHUNK, DN_HEAD_DIM
    n_grp = s_ref.shape[0]
    gw = n_grp * d
    hg = pl.program_id(1)
    c = pl.program_id(2)
    halo = SUBLANES

    @pl.when(c == 0)
    def _():
        s_ref[...] = jnp.zeros_like(s_ref)
        for buf in (qbuf, kbuf, vbuf):
            buf[0:halo, :] = jnp.zeros((halo, gw), F32)

    @pl.when(c > 0)
    def _():
        for buf in (qbuf, kbuf, vbuf):
            buf[0:halo, :] = buf[c_len:c_len + halo, :]

    def conv_silu(buf, x_ref, cw_ref):
        buf[halo:halo + c_len, :] = x_ref[...]
        acc = jnp.zeros((c_len, gw), F32)
        for j in range(DN_CONV):
            off = halo - (DN_CONV - 1) + j
            acc = acc + cw_ref[j:j + 1, :] * buf[off:off + c_len, :]
        return _silu(acc)

    q_all = conv_silu(qbuf, q_ref, cwq_ref)
    k_all = conv_silu(kbuf, k_ref, cwk_ref)
    v_all = conv_silu(vbuf, v_ref, cwv_ref)
    gate_all = nw_ref[...] * _silu(z_ref[...])

    ba = ba_ref[...]
    nh = pl.num_programs(1) * n_grp
    lane = _iota2((c_len, LANES), 1)
    row64 = _iota2((c_len, c_len), 0)
    col64 = _iota2((c_len, c_len), 1)
    tri_incl = (row64 >= col64).astype(BF16)
    causal = row64 >= col64
    strict = row64 > col64
    g_all = -jnp.exp(alog_ref[...]) * _softplus(ba + dtb_ref[...])
    gc_all = _dot_split(g_all, tri_incl, 3, ones_left=True)

    grp = list(range(n_grp))
    sls = [slice(gi * d, (gi + 1) * d) for gi in grp]

    def l2n(x):
        return x * lax.rsqrt(jnp.sum(x * x, axis=1, keepdims=True) + 1e-6)

    def pick(mat, lane_idx):
        return jnp.sum(jnp.where(lane == lane_idx, mat, 0.0), axis=1, keepdims=True)

    def decay_of(gc):
        gc_col = jnp.broadcast_to(gc, (c_len, c_len))
        gc_row = jnp.sum(jnp.where(row64 == col64, gc_col, 0.0), axis=0, keepdims=True)
        return jnp.exp(jnp.where(causal, gc_col - gc_row, NEG_BIG))

    q = [l2n(q_all[:, sl]) * (d ** -0.5) for sl in sls]
    k = [l2n(k_all[:, sl]) for sl in sls]
    v = [v_all[:, sl] for sl in sls]
    beta = [_sigmoid(pick(ba, hg * n_grp + gi)) for gi in grp]
    gc = [pick(gc_all, nh + hg * n_grp + gi) for gi in grp]
    gc_last = [x[c_len - 1:c_len, :] for x in gc]
    decay = _each(decay_of, gc)
    eg = _each(jnp.exp, gc)
    kb = _each(lambda a, b: a * b, k, beta)
    kkt = _each(_dot_nt, kb, k)
    low = _each(lambda m, dc: jnp.where(strict, m * dc, 0.0), kkt, decay)
    t_inv = _unit_lower_inverse(low, c_len, c_len)
    u = _each(lambda t, a, b: _dot(t, a * b), t_inv, v, beta)
    w = _each(lambda t, a, e: _dot(t, a * e), t_inv, kb, eg)
    qk = _each(lambda a, b, dc: _dot_nt(a, b) * dc, q, k, decay)
    s = [s_ref[gi] for gi in grp]
    ws = _each(_dot, w, s)
    v_new = _each(lambda a, b: a - b, u, ws)
    qs = _each(lambda a, e, st: _dot(a * e, st), q, eg, s)
    intra = _each(_dot, qk, v_new)
    kv = _each(lambda a, gl, g1, vn: _dot_tn(a * jnp.exp(gl - g1), vn), k, gc_last, gc, v_new)
    for gi in grp:
        s_ref[gi] = s[gi] * jnp.exp(gc_last[gi]) + kv[gi]
        out = qs[gi] + intra[gi]
        out = out * lax.rsqrt(jnp.mean(out * out, axis=1, keepdims=True) + DN_NORM_EPS)
        o_ref[:, sls[gi]] = (out * gate_all[:, sls[gi]]).astype(o_ref.dtype)


def _gated_deltanet(proj, tail, ba_blk, conv_w, alog_pad, dtb_pad, norm_w, bsz, seq, n_heads, n_grp):
    t_tot = bsz * seq
    nc = seq // CHUNK
    d = DN_HEAD_DIM
    assert n_heads % n_grp == 0
    n_hg = n_heads // n_grp
    gw = n_grp * d

    def col(group):
        return pl.BlockSpec((CHUNK, gw), lambda b, h, c, g=group: (b * nc + c, g * n_hg + h))

    def cw(group):
        return pl.BlockSpec((DN_CONV, gw), lambda b, h, c, g=group: (0, g * n_hg + h))

    vec = pl.BlockSpec((1, LANES), lambda b, h, c: (0, 0))
    return pl.pallas_call(
        _dn_kernel, grid=(bsz, n_hg, nc),
        in_specs=[col(0), col(1), col(2), col(3),
                  pl.BlockSpec((CHUNK, LANES), lambda b, h, c: (b * nc + c, ba_blk)),
                  cw(0), cw(1), cw(2), vec, vec,
                  pl.BlockSpec((1, gw), lambda b, h, c: (0, 0))],
        out_specs=pl.BlockSpec((CHUNK, gw), lambda b, h, c: (b * nc + c, h)),
        out_shape=jax.ShapeDtypeStruct((t_tot, n_heads * d), BF16),
        scratch_shapes=[pltpu.VMEM((CHUNK + SUBLANES, gw), F32)] * 3 + [pltpu.VMEM((n_grp, d, d), F32)],
        compiler_params=_cparams(("arbitrary", "arbitrary", "arbitrary")), name="gated_deltanet",
    )(proj, proj, proj, proj, tail, conv_w, conv_w, conv_w, alog_pad, dtb_pad, jnp.tile(norm_w, (1, n_grp)))


def _rw_kernel(has_vres, *refs):
    if has_vres:
        (r_ref, k_ref, v_ref, lo_ref, vf_ref, mur_ref, muk_ref, muv_ref, mulo_ref, w0_ref, a0_ref,
         kk_ref, ka_ref, rk_ref, gw_ref, gb_ref, w2_ref, a2_ref, g2_ref, v0_ref, v2_ref,
         o_ref, rbuf, kbuf, vbuf, lobuf, ht_ref) = refs
    else:
        (r_ref, k_ref, v_ref, lo_ref, mur_ref, muk_ref, muv_ref, mulo_ref, w0_ref, a0_ref,
         kk_ref, ka_ref, rk_ref, gw_ref, gb_ref, w2_ref, a2_ref, g2_ref,
         o_ref, vf_out_ref, rbuf, kbuf, vbuf, lobuf, ht_ref) = refs
    c_len, n = CHUNK, RW_HEAD_DIM
    n_grp = ht_ref.shape[0]
    c = pl.program_id(2)
    halo = SUBLANES
    bufs = (rbuf, kbuf, vbuf, lobuf)

    @pl.when(c == 0)
    def _():
        ht_ref[...] = jnp.zeros_like(ht_ref)
        for buf in bufs:
            buf[0:halo, :] = jnp.zeros((halo, buf.shape[1]), F32)

    @pl.when(c > 0)
    def _():
        for buf in bufs:
            buf[0:halo, :] = buf[c_len:c_len + halo, :]

    def shift_mix(buf, x_ref, mu_ref):
        x = x_ref[...]
        buf[halo:halo + c_len, :] = x
        prev = buf[halo - 1:halo - 1 + c_len, :]
        return x + (prev - x) * mu_ref[...]

    r_all = shift_mix(rbuf, r_ref, mur_ref)
    k_all = shift_mix(kbuf, k_ref, muk_ref)
    v_all = shift_mix(vbuf, v_ref, muv_ref)
    lo = shift_mix(lobuf, lo_ref, mulo_ref)
    w_pre = w0_ref[...] + _dot(jnp.tanh(lo[:, 0:128]), w2_ref[...])
    w_log = -_softplus(-w_pre) - 0.5
    logd_all = -jnp.exp(w_log)
    a_all = _sigmoid(a0_ref[...] + _dot(lo[:, 128:256], a2_ref[...]))
    g_all = _dot(_sigmoid(lo[:, 256:512]), g2_ref[...])
    if has_vres:
        v_all = v_all + (vf_ref[...] - v_all) * _sigmoid(v0_ref[...] + _dot(lo[:, 512:640], v2_ref[...]))
    else:
        vf_out_ref[...] = v_all
    kk_all = k_all * kk_ref[...]
    k_all = k_all * (1.0 + (a_all - 1.0) * ka_ref[...])
    rk_all = r_all * k_all * rk_ref[...]

    row64 = _iota2((c_len, c_len), 0)
    col64 = _iota2((c_len, c_len), 1)
    tri_incl = (row64 >= col64).astype(BF16)
    gcum_all = _dot_split(logd_all, tri_incl, 3, ones_left=True)
    p_incl_all = jnp.exp(gcum_all)
    p_prev_all = jnp.exp(gcum_all - logd_all)
    p_inv_all = jnp.exp(-gcum_all)

    r128 = _iota2((LANES, LANES), 0)
    c128 = _iota2((LANES, LANES), 1)
    same_head = (r128 >> 6) == (c128 >> 6)
    seg_ones = same_head.astype(BF16)
    strict = r128 > c128
    incl = r128 >= c128
    head0 = _iota2((c_len, LANES), 1) < n

    def seg_sum(x):
        return _dot_split(x, seg_ones, 2)

    def stack(x):
        return jnp.concatenate([jnp.where(head0, x, 0.0), jnp.where(head0, 0.0, x)], axis=0)

    grp = list(range(n_grp))
    sls = [slice(gi * LANES, (gi + 1) * LANES) for gi in grp]

    def cut(x):
        return [x[:, sl] for sl in sls]

    r, k, v, a = cut(r_all), cut(k_all), cut(v_all), cut(a_all)
    p_incl, p_prev, p_inv = cut(p_incl_all), cut(p_prev_all), cut(p_inv_all)
    kk = cut(kk_all)
    kk_ss = _each(lambda x: seg_sum(x * x), kk)
    kk = _each(lambda x, ss: x * lax.rsqrt(ss + 1e-12), kk, kk_ss)
    a_s = _each(lambda x, p: stack(-x * p), kk, p_prev)
    r_s = _each(lambda x, p: stack(x * p), r, p_incl)
    b_s = _each(lambda x, y, p: stack(x * y * p), kk, a, p_inv)
    k_s = _each(lambda x, p: stack(x * p), k, p_inv)
    v_s = _each(stack, v)

    a_ab = _each(_dot_nt, a_s, b_s)
    a_ak = _each(_dot_nt, a_s, k_s)
    a_rb = _each(_dot_nt, r_s, b_s)
    a_rk = _each(_dot_nt, r_s, k_s)
    t_inv = _unit_lower_inverse(_each(lambda m: jnp.where(strict, -m, 0.0), a_ab), 2 * c_len, c_len)
    akv = _each(lambda m, x: _dot(jnp.where(strict, m, 0.0), x), a_ak, v_s)
    u0 = _each(_dot, t_inv, akv)
    w = _each(_dot, t_inv, a_s)

    ht = [ht_ref[gi] for gi in grp]
    wh = _each(_dot_nt, w, ht)
    u = _each(lambda x, y: x + y, wh, u0)
    y_h = _each(_dot_nt, r_s, ht)
    y_u = _each(lambda m, x: _dot(jnp.where(incl, m, 0.0), x), a_rb, u)
    y_v = _each(lambda m, x: _dot(jnp.where(incl, m, 0.0), x), a_rk, v_s)
    h_u = _each(_dot_tn, u, b_s)
    h_v = _each(_dot_tn, v_s, k_s)
    y = []
    for gi in grp:
        p_last = p_incl[gi][c_len - 1:c_len, :]
        ht_ref[gi] = jnp.where(same_head, (ht[gi] + h_u[gi] + h_v[gi]) * p_last, 0.0)
        y_s = y_h[gi] + y_u[gi] + y_v[gi]
        y.append(y_s[0:c_len, :] + y_s[c_len:2 * c_len, :])

    mu = _each(lambda x: seg_sum(x) * (1.0 / n), y)
    yc = _each(lambda x, m: x - m, y, mu)
    var = _each(lambda x: seg_sum(x * x) * (1.0 / n), yc)
    bonus = _each(seg_sum, cut(rk_all))
    for gi in grp:
        sl = sls[gi]
        yn = yc[gi] * lax.rsqrt(var[gi] + RW_GN_EPS) * gw_ref[:, sl] + gb_ref[:, sl]
        o_ref[:, sl] = ((yn + bonus[gi] * v[gi]) * g_all[:, sl]).astype(o_ref.dtype)


def _rwkv7(proj, tail, r_blk0, v_first, prm, bsz, seq, n_pairs, n_grp):
    t_tot = bsz * seq
    nc = seq // CHUNK
    w_rw = n_pairs * LANES
    gw = n_grp * LANES
    assert n_pairs % n_grp == 0 and r_blk0 % n_grp == 0
    n_pg = n_pairs // n_grp
    has_vres = v_first is not None

    def col(group):
        return pl.BlockSpec((CHUNK, gw), lambda b, p, c, g=group: (b * nc + c, r_blk0 // n_grp + g * n_pg + p))

    tok = pl.BlockSpec((CHUNK, gw), lambda b, p, c: (b * nc + c, p))
    lo_spec = pl.BlockSpec((CHUNK, 5 * LANES), lambda b, p, c: (b * nc + c, 0))
    vec = pl.BlockSpec((1, gw), lambda b, p, c: (0, p))
    vec_lo = pl.BlockSpec((1, 5 * LANES), lambda b, p, c: (0, 0))

    def lowrank(rank):
        return pl.BlockSpec((rank, gw), lambda b, p, c: (0, p))

    in_specs = [col(0), col(1), col(2), lo_spec]
    args = [proj, proj, proj, tail]
    if has_vres:
        in_specs.append(tok)
        args.append(v_first)
    in_specs += [vec, vec, vec, vec_lo] + [vec] * 7 + [lowrank(128), lowrank(128), lowrank(256)]
    args += [prm["mu_r"], prm["mu_k"], prm["mu_v"], prm["mu_lo"], prm["w0"], prm["a0"], prm["k_k"],
             prm["k_a"], prm["r_k"], prm["gn_w"], prm["gn_b"], prm["w2"], prm["a2"], prm["g2"]]
    if has_vres:
        in_specs += [vec, lowrank(128)]
        args += [prm["v0"], prm["v2"]]
    out_shape = [jax.ShapeDtypeStruct((t_tot, w_rw), BF16)]
    out_specs = [tok]
    if not has_vres:
        out_shape.append(jax.ShapeDtypeStruct((t_tot, w_rw), F32))
        out_specs.append(tok)
    halo_rows = CHUNK + SUBLANES
    outs = pl.pallas_call(
        functools.partial(_rw_kernel, has_vres), grid=(bsz, n_pg, nc),
        in_specs=in_specs, out_specs=out_specs, out_shape=out_shape,
        scratch_shapes=[pltpu.VMEM((halo_rows, gw), F32)] * 3
        + [pltpu.VMEM((halo_rows, 5 * LANES), F32), pltpu.VMEM((n_grp, LANES, LANES), F32)],
        compiler_params=_cparams(("arbitrary", "arbitrary", "arbitrary")), name="rwkv7",
    )(*args)
    if has_vres:
        return outs[0], v_first
    return outs[0], outs[1]


def _layer_norm_rows(y, g, b):
    mu = jnp.mean(y, axis=1, keepdims=True)
    yc = y - mu
    var = jnp.mean(yc * yc, axis=1, keepdims=True)
    return yc * lax.rsqrt(var + LN_EPS) * g + b


def _ln_router_kernel(y_ref, g_ref, b_ref, rw_ref, rb_ref, x_ref, ti_ref, tw_ref, pos_ref, cnt_ref,
                      base_ref):
    tm = y_ref.shape[0]
    i = pl.program_id(0)

    @pl.when(i == 0)
    def _():
        base_ref[...] = jnp.zeros_like(base_ref)

    x = _layer_norm_rows(y_ref[...], g_ref[...], b_ref[...])
    x_ref[...] = x
    logits = _dot(x, rw_ref[...]) + rb_ref[...]
    lane = _iota2((tm, LANES), 1)
    lane_f = lane.astype(F32)
    work = logits
    sels, vals, idxs = [], [], []
    for _ in range(TOP_K):
        m = jnp.max(work, axis=1, keepdims=True)
        idx = jnp.min(jnp.where(work == m, lane_f, float(LANES)), axis=1, keepdims=True)
        sel = lane_f == idx
        sels.append(sel)
        vals.append(m)
        idxs.append(idx)
        work = jnp.where(sel, 2.0 * NEG_BIG, work)
    exps = [jnp.exp(m - vals[0]) for m in vals]
    denom = exps[0] + exps[1] + exps[2] + exps[3]
    cnt = jnp.zeros((tm, LANES), F32)
    for sel in sels:
        cnt = cnt + sel.astype(F32)
    rowt = _iota2((tm, tm), 0)
    colt = _iota2((tm, tm), 1)
    before = _dot((rowt > colt).astype(F32), cnt)
    posmat = base_ref[...] + before
    ti = jnp.zeros((tm, LANES), F32)
    tw = jnp.zeros((tm, LANES), F32)
    pos = jnp.zeros((tm, LANES), F32)
    for kk in range(TOP_K):
        here = lane == kk
        ti = jnp.where(here, idxs[kk], ti)
        tw = jnp.where(here, exps[kk] / denom, tw)
        pos = jnp.where(here, jnp.sum(jnp.where(sels[kk], posmat, 0.0), axis=1, keepdims=True), pos)
    ti_ref[...] = ti.astype(I32)
    tw_ref[...] = tw
    pos_ref[...] = pos.astype(I32)
    base_new = base_ref[...] + jnp.sum(cnt, axis=0, keepdims=True)
    base_ref[...] = base_new
    cnt_ref[...] = jnp.broadcast_to(base_new, cnt_ref.shape).astype(I32)


def _ln_router(y, g, b, rw_pad, rb_pad, tm):
    t_tot, dm = y.shape
    tm = min(tm, t_tot)
    assert t_tot % tm == 0
    row = pl.BlockSpec((tm, dm), lambda i: (i, 0))
    vec = pl.BlockSpec((1, dm), lambda i: (0, 0))
    small = pl.BlockSpec((tm, LANES), lambda i: (i, 0))
    return pl.pallas_call(
        _ln_router_kernel, grid=(t_tot // tm,),
        in_specs=[row, vec, vec, pl.BlockSpec((dm, LANES), lambda i: (0, 0)),
                  pl.BlockSpec((1, LANES), lambda i: (0, 0))],
        out_specs=[row, small, small, small, pl.BlockSpec((SUBLANES, LANES), lambda i: (0, 0))],
        out_shape=[jax.ShapeDtypeStruct((t_tot, dm), F32),
                   jax.ShapeDtypeStruct((t_tot, LANES), I32),
                   jax.ShapeDtypeStruct((t_tot, LANES), F32),
                   jax.ShapeDtypeStruct((t_tot, LANES), I32),
                   jax.ShapeDtypeStruct((SUBLANES, LANES), I32)],
        scratch_shapes=[pltpu.VMEM((1, LANES), F32)],
        compiler_params=_cparams(("arbitrary",)), name="ln_router",
    )(y, g, b, rw_pad, rb_pad)


def _row_gather_start(src3, dst3, sem, idx_of_row, n_rows):
    def body(g, carry):
        for j in range(SUBLANES):
            idx = idx_of_row(g * SUBLANES + j)
            pltpu.make_async_copy(src3.at[idx >> 3, pl.ds(idx & (SUBLANES - 1), 1)],
                                  dst3.at[g, pl.ds(j, 1)], sem).start()
        return carry
    assert n_rows % SUBLANES == 0
    lax.fori_loop(0, n_rows // SUBLANES, body, 0)


def _row_gather_wait(dst3, sem):
    pltpu.make_async_copy(dst3, dst3, sem).wait()


def _moe_up_kernel(te_ref, nu_ref, tos_ref, x_hbm, wg_ref, wu_ref, bg_ref, bu_ref, h_ref, xbuf, sem):
    tm = h_ref.shape[0]
    i = pl.program_id(0)
    slot = lax.rem(i, 2)
    n_used = nu_ref[0]

    def start(tile, s):
        _row_gather_start(x_hbm, xbuf.at[s], sem.at[s], lambda r: tos_ref[tile * tm + r], tm)

    @pl.when(i == 0)
    def _():
        start(0, 0)

    @pl.when(i + 1 < n_used)
    def _():
        start(i + 1, 1 - slot)

    @pl.when(i < n_used)
    def _():
        _row_gather_wait(xbuf.at[slot], sem.at[slot])
        x = xbuf[slot].reshape(tm, xbuf.shape[-1]).astype(BF16)
        gate = jnp.dot(x, wg_ref[...], preferred_element_type=F32) + bg_ref[...]
        up = jnp.dot(x, wu_ref[...], preferred_element_type=F32) + bu_ref[...]
        gate = jnp.minimum(gate, SWIGLU_LIMIT)
        up = jnp.clip(up, -SWIGLU_LIMIT, SWIGLU_LIMIT)
        h_ref[...] = ((up + 1.0) * gate * _sigmoid(gate * SWIGLU_ALPHA)).astype(h_ref.dtype)

    @pl.when(i >= n_used)
    def _():
        h_ref[...] = jnp.zeros_like(h_ref)


def _moe_down_kernel(te_ref, nu_ref, h_ref, wd_ref, bd_ref, y_ref, wbf):
    i = pl.program_id(0)
    new_expert = jnp.logical_or(i == 0, te_ref[i] != te_ref[jnp.maximum(i - 1, 0)])

    @pl.when(jnp.logical_and(new_expert, i < nu_ref[0]))
    def _():
        wbf[...] = wd_ref[...].astype(BF16)

    @pl.when(i < nu_ref[0])
    def _():
        y_ref[...] = (jnp.dot(h_ref[...], wbf[...], preferred_element_type=F32) + bd_ref[...]).astype(y_ref.dtype)

    @pl.when(i >= nu_ref[0])
    def _():
        y_ref[...] = jnp.zeros_like(y_ref)


def _moe_experts(x, tile_expert, n_used, tok_of_slot, wg, wu, bg, bu, wd_all, layer, bd, tm):
    n_slots = tok_of_slot.shape[0]
    n_tiles = n_slots // tm
    n_exp, dm, de = wg.shape
    h = pl.pallas_call(
        _moe_up_kernel,
        grid_spec=pltpu.PrefetchScalarGridSpec(
            num_scalar_prefetch=3, grid=(n_tiles,),
            in_specs=[pl.BlockSpec(memory_space=pl.ANY),
                      pl.BlockSpec((None, dm, de), lambda i, te, nu, tos: (te[i], 0, 0)),
                      pl.BlockSpec((None, dm, de), lambda i, te, nu, tos: (te[i], 0, 0)),
                      pl.BlockSpec((None, 1, de), lambda i, te, nu, tos: (te[i], 0, 0)),
                      pl.BlockSpec((None, 1, de), lambda i, te, nu, tos: (te[i], 0, 0))],
            out_specs=pl.BlockSpec((tm, de), lambda i, te, nu, tos: (i, 0)),
            scratch_shapes=[pltpu.VMEM((2, tm // SUBLANES, SUBLANES, dm), F32),
                            pltpu.SemaphoreType.DMA((2,))]),
        out_shape=jax.ShapeDtypeStruct((n_slots, de), BF16),
        compiler_params=_cparams(("arbitrary",)), name="moe_gate_up",
    )(tile_expert, n_used, tok_of_slot, x.reshape(-1, SUBLANES, dm), wg, wu, bg, bu)
    return pl.pallas_call(
        _moe_down_kernel,
        grid_spec=pltpu.PrefetchScalarGridSpec(
            num_scalar_prefetch=2, grid=(n_tiles,),
            in_specs=[pl.BlockSpec((tm, de), lambda i, te, nu: (i, 0)),
                      pl.BlockSpec((None, None, de, dm), lambda i, te, nu: (layer, te[i], 0, 0)),
                      pl.BlockSpec((None, 1, dm), lambda i, te, nu: (te[i], 0, 0))],
            out_specs=pl.BlockSpec((tm, dm), lambda i, te, nu: (i, 0)),
            scratch_shapes=[pltpu.VMEM((de, dm), BF16)]),
        out_shape=jax.ShapeDtypeStruct((n_slots, dm), F32),
        compiler_params=_cparams(("arbitrary",)), name="moe_down",
    )(tile_expert, n_used, h, wd_all, bd)


def _combine_ln_kernel(alpha, dest_ref, ys_hbm, x_ref, tw_ref, g_ref, b_ref, o_ref, ob_ref, ybuf, sem):
    tm = x_ref.shape[0]
    i = pl.program_id(0)
    n = pl.num_programs(0)
    slot = lax.rem(i, 2)

    def start(tile, s):
        for kk in range(TOP_K):
            _row_gather_start(ys_hbm, ybuf.at[s, kk], sem.at[s],
                              lambda r, kk=kk: dest_ref[(tile * tm + r) * TOP_K + kk], tm)

    @pl.when(i == 0)
    def _():
        start(0, 0)

    @pl.when(i + 1 < n)
    def _():
        start(i + 1, 1 - slot)

    for kk in range(TOP_K):
        _row_gather_wait(ybuf.at[slot, kk], sem.at[slot])
    tw = tw_ref[...]
    acc = alpha * x_ref[...]
    for kk in range(TOP_K):
        acc = acc + tw[:, kk:kk + 1] * ybuf[slot, kk].reshape(tm, ybuf.shape[-1])
    out = _layer_norm_rows(acc, g_ref[...], b_ref[...])
    o_ref[...] = out
    ob_ref[...] = out.astype(BF16)


def _combine_ln(ys, x, tw, dest_flat, g, b, alpha, tm):
    t_tot, dm = x.shape
    tm = min(tm, t_tot)
    assert t_tot % tm == 0
    row = pl.BlockSpec((tm, dm), lambda i, d: (i, 0))
    vec = pl.BlockSpec((1, dm), lambda i, d: (0, 0))
    return pl.pallas_call(
        functools.partial(_combine_ln_kernel, alpha),
        grid_spec=pltpu.PrefetchScalarGridSpec(
            num_scalar_prefetch=1, grid=(t_tot // tm,),
            in_specs=[pl.BlockSpec(memory_space=pl.ANY), row,
                      pl.BlockSpec((tm, LANES), lambda i, d: (i, 0)), vec, vec],
            out_specs=[row, row],
            scratch_shapes=[pltpu.VMEM((2, TOP_K, tm // SUBLANES, SUBLANES, dm), F32),
                            pltpu.SemaphoreType.DMA((2,))]),
        out_shape=[jax.ShapeDtypeStruct((t_tot, dm), F32), jax.ShapeDtypeStruct((t_tot, dm), BF16)],
        compiler_params=_cparams(("arbitrary",)), name="moe_combine_ln",
    )(dest_flat, ys.reshape(-1, SUBLANES, dm), x, tw, g, b)


def _tiles(t_tot):
    return dict(mm_m=min(1024, t_tot), mm_n=1024, mmw_n=512, ln_m=min(256, t_tot), moe_m=min(512, t_tot),
                comb_m=min(64, t_tot), rec_grp=16)


def _pad_cols(w, width):
    return jnp.pad(w, ((0, 0), (0, width - w.shape[1])))


def _pad_rows(w, rows):
    return jnp.pad(w, ((0, rows - w.shape[0]), (0, 0)))


def kernel(x, mix_w_in, mix_w_in_vres, dn_conv_w, dn_a_log, dn_dt_bias, dn_norm_w, rw_shift_mu, rw_shift_mu_vres, rw_w0, rw_w2, rw_a0, rw_a2, rw_g2, rw_v0, rw_v2, rw_k_k, rw_k_a, rw_r_k, rw_gn_w, rw_gn_b, mix_w_out, ln1_g, ln1_b, router_w, router_b, exp_w_gate, exp_b_gate, exp_w_up, exp_b_up, exp_w_down, exp_b_down, ln2_g, ln2_b):
    bsz, seq, dm = x.shape
    depth = mix_w_in.shape[0]
    t_tot = bsz * seq
    dn_heads = dn_a_log.shape[1]
    w_dn = dn_heads * DN_HEAD_DIM
    w_rw = rw_w0.shape[1]
    n_pairs = w_rw // LANES
    r_dec, r_aaa, r_gate, r_mv = rw_w2.shape[1], rw_a2.shape[1], rw_g2.shape[1], rw_v2.shape[1]
    n_exp = router_w.shape[2]
    assert seq % CHUNK == 0 and w_rw % LANES == 0 and 2 * dn_heads <= LANES
    assert max(r_dec, r_aaa, r_mv) <= LANES and r_gate == 2 * LANES and n_exp <= LANES
    alpha = (2 * depth) ** 0.25
    tl = _tiles(t_tot)
    n_dn_in = 4 * w_dn + 2 * dn_heads

    x2d = x.reshape(t_tot, dm)
    xb = x2d.astype(BF16)
    v_first = None
    for l in range(depth):
        w_in = mix_w_in[l]
        rw0 = n_dn_in
        o_lo = rw0 + 3 * w_rw
        v_lo_w = mix_w_in_vres[l - 1] if l > 0 else jnp.zeros((dm, r_mv), F32)
        w_tail = jnp.concatenate([
            _pad_cols(w_in[:, o_lo:o_lo + r_dec], LANES),
            _pad_cols(w_in[:, o_lo + r_dec:o_lo + r_dec + r_aaa], LANES),
            w_in[:, o_lo + r_dec + r_aaa:o_lo + r_dec + r_aaa + r_gate],
            _pad_cols(v_lo_w, LANES),
            _pad_cols(w_in[:, 4 * w_dn:n_dn_in], LANES)], axis=1).astype(BF16)
        proj_dn = _matmul_wcast([xb], mix_w_in, l, 4 * w_dn, tl["mm_m"], tl["mmw_n"], F32, name="in_proj_dn")
        proj_rw = _matmul_wcast([xb], w_in[:, rw0:o_lo], 0, 3 * w_rw, tl["mm_m"], tl["mmw_n"], F32,
                                name="in_proj_rw")
        tail = _matmul(xb, w_tail, tl["mm_m"], w_tail.shape[1], F32, name="in_proj_tail")

        zeros_h = jnp.zeros((dn_heads,), F32)
        alog_pad = _pad_cols(jnp.concatenate([zeros_h, dn_a_log[l]])[None, :], LANES)
        dtb_pad = _pad_cols(jnp.concatenate([zeros_h, dn_dt_bias[l]])[None, :], LANES)
        o_dn = _gated_deltanet(proj_dn, tail, 5, dn_conv_w[l], alog_pad, dtb_pad, dn_norm_w[l][None, :],
                               bsz, seq, dn_heads, min(tl["rec_grp"], dn_heads))

        mu = rw_shift_mu[l]
        mu_lo_parts = [_pad_cols(mu[None, 3 * w_rw:3 * w_rw + r_dec], LANES),
                       _pad_cols(mu[None, 3 * w_rw + r_dec:3 * w_rw + r_dec + r_aaa], LANES),
                       mu[None, 3 * w_rw + r_dec + r_aaa:],
                       _pad_cols(rw_shift_mu_vres[l - 1][None, :] if l > 0 else jnp.zeros((1, r_mv), F32), LANES)]
        prm = dict(mu_r=mu[None, :w_rw], mu_k=mu[None, w_rw:2 * w_rw], mu_v=mu[None, 2 * w_rw:3 * w_rw],
                   mu_lo=jnp.concatenate(mu_lo_parts, axis=1),
                   w0=rw_w0[l][None, :], a0=rw_a0[l][None, :], k_k=rw_k_k[l][None, :], k_a=rw_k_a[l][None, :],
                   r_k=rw_r_k[l].reshape(1, w_rw), gn_w=rw_gn_w[l][None, :], gn_b=rw_gn_b[l][None, :],
                   w2=_pad_rows(rw_w2[l], LANES), a2=_pad_rows(rw_a2[l], LANES), g2=rw_g2[l])
        if l > 0:
            prm["v0"] = rw_v0[l - 1][None, :]
            prm["v2"] = _pad_rows(rw_v2[l - 1], LANES)
        o_rw, v_first = _rwkv7(proj_rw, tail, 0, v_first if l > 0 else None, prm, bsz, seq, n_pairs,
                               min(tl["rec_grp"], n_pairs))

        y1 = _matmul_wcast([o_dn, o_rw], mix_w_out, l, dm, tl["mm_m"], tl["mmw_n"], F32, res=x2d, alpha=alpha,
                           name="out_proj")
        rw_pad = _pad_cols(router_w[l], LANES)
        rb_pad = jnp.concatenate([router_b[l], jnp.full((LANES - n_exp,), NEG_BIG, F32)])[None, :]
        x1, top_i, top_w, pos, cnt = _ln_router(y1, ln1_g[l][None, :], ln1_b[l][None, :], rw_pad, rb_pad,
                                                tl["ln_m"])

        tm = tl["moe_m"]
        counts = cnt[0, :n_exp]
        padded = ((counts + tm - 1) // tm) * tm
        ends = jnp.cumsum(padded)
        offs = ends - padded
        n_tiles = (t_tot * TOP_K) // tm + n_exp
        dest = offs[top_i[:, :TOP_K]] + pos[:, :TOP_K]
        tok_ids = jnp.broadcast_to(jnp.arange(t_tot, dtype=I32)[:, None], (t_tot, TOP_K))
        tok_of_slot = jnp.zeros((n_tiles * tm,), I32).at[dest.reshape(-1)].set(tok_ids.reshape(-1))
        tile_start = jnp.arange(n_tiles, dtype=I32) * tm
        tile_expert = jnp.minimum(jnp.searchsorted(ends, tile_start, side="right"), n_exp - 1).astype(I32)
        n_used = (ends[-1] // tm).astype(I32)[None]

        ys = _moe_experts(x1, tile_expert, n_used, tok_of_slot,
                          exp_w_gate[l].astype(BF16), exp_w_up[l].astype(BF16),
                          exp_b_gate[l][:, None, :], exp_b_up[l][:, None, :],
                          exp_w_down, l, exp_b_down[l][:, None, :], tm)
        x2d, xb = _combine_ln(ys, x1, top_w, dest.reshape(-1), ln2_g[l][None, :], ln2_b[l][None, :],
                              alpha, tl["comb_m"])
    return x2d.reshape(bsz, seq, dm)
```

```python
import functools
import math

import jax
import jax.numpy as jnp
from jax import lax
from jax.experimental import pallas as pl
from jax.experimental.pallas import tpu as pltpu

F32 = jnp.float32
BF16 = jnp.bfloat16
I32 = jnp.int32

LANES = 128
SUBLANES = 8
VMEM_LIMIT = 56 * 1024 * 1024

CHUNK = 64
DN_HEAD_DIM = 128
DN_CONV = 4
DN_NORM_EPS = 1e-6
RW_HEAD_DIM = 64
RW_GN_EPS = 64e-5
TOP_K = 4
SWIGLU_ALPHA = 1.702
SWIGLU_LIMIT = 7.0
LN_EPS = 1e-5
NEG_BIG = -1e30


def _dot(a, b):
    return jnp.dot(a.astype(BF16), b.astype(BF16), preferred_element_type=F32)


def _dot_nt(a, b):
    return lax.dot_general(a.astype(BF16), b.astype(BF16), (((1,), (1,)), ((), ())),
                           preferred_element_type=F32)


def _dot_tn(a, b):
    return lax.dot_general(a.astype(BF16), b.astype(BF16), (((0,), (0,)), ((), ())),
                           preferred_element_type=F32)


def _dot_split(a, ones, passes, ones_left=False):
    acc, rem = None, a
    for p in range(passes):
        term = rem.astype(BF16)
        part = (jnp.dot(ones, term, preferred_element_type=F32) if ones_left
                else jnp.dot(term, ones, preferred_element_type=F32))
        acc = part if acc is None else acc + part
        if p + 1 < passes:
            rem = rem - term.astype(F32)
    return acc


def _iota2(shape, dim):
    return lax.broadcasted_iota(I32, shape, dim)


def _sigmoid(x):
    return 1.0 / (1.0 + jnp.exp(-x))


def _softplus(x):
    return jnp.maximum(x, 0.0) + jnp.log(1.0 + jnp.exp(-jnp.abs(x)))


def _silu(x):
    return x * _sigmoid(x)


def _each(fn, *lists):
    return [fn(*args) for args in zip(*lists)]


def _unit_lower_inverse(lows, n, blk):
    row = _iota2((n, n), 0)
    col = _iota2((n, n), 1)
    eye = (row == col).astype(F32)
    pair = (row >> 1) == (col >> 1)
    invs = _each(lambda low: eye - jnp.where(pair, low, 0.0), lows)
    shift = 1
    while (1 << shift) < blk:
        same_big = (row >> (shift + 1)) == (col >> (shift + 1))
        diff_small = (row >> shift) != (col >> shift)
        mask = jnp.logical_and(same_big, diff_small)
        cms = _each(lambda low: jnp.where(mask, low, 0.0), lows)
        tmp = _each(_dot, invs, cms)
        upd = _each(_dot, tmp, invs)
        invs = _each(lambda inv, d: inv - d, invs, upd)
        shift += 1
    return invs


def _cparams(sem):
    return pltpu.CompilerParams(dimension_semantics=sem, vmem_limit_bytes=VMEM_LIMIT)


def _mm_kernel(x_ref, w_ref, o_ref):
    o_ref[...] = jnp.dot(x_ref[...], w_ref[...], preferred_element_type=F32).astype(o_ref.dtype)


def _mm_res_kernel(alpha, x_ref, w_ref, r_ref, o_ref):
    acc = jnp.dot(x_ref[...], w_ref[...], preferred_element_type=F32)
    o_ref[...] = (alpha * r_ref[...] + acc).astype(o_ref.dtype)


def _matmul(x, w, tm, tn, out_dtype, res=None, alpha=1.0, name="matmul"):
    m, k = x.shape
    n = w.shape[1]
    tm = min(tm, m)
    tn = max(t for t in range(LANES, min(tn, n) + 1, LANES) if n % t == 0)
    assert m % tm == 0 and n % LANES == 0
    in_specs = [pl.BlockSpec((tm, k), lambda i, j: (i, 0)),
                pl.BlockSpec((k, tn), lambda i, j: (0, j))]
    args = [x, w]
    body = _mm_kernel
    if res is not None:
        in_specs.append(pl.BlockSpec((tm, tn), lambda i, j: (i, j)))
        args.append(res)
        body = functools.partial(_mm_res_kernel, alpha)
    return pl.pallas_call(
        body, grid=(m // tm, n // tn), in_specs=in_specs,
        out_specs=pl.BlockSpec((tm, tn), lambda i, j: (i, j)),
        out_shape=jax.ShapeDtypeStruct((m, n), out_dtype),
        compiler_params=_cparams(("arbitrary", "arbitrary")), name=name)(*args)


def _mm_wcast_kernel(n_lhs, alpha, has_res, w_is_t, *refs):
    lhs = refs[:n_lhs]
    w_ref = refs[n_lhs]
    r_ref = refs[n_lhs + 1] if has_res else None
    o_ref = refs[n_lhs + 1 + int(has_res)]
    wbf = refs[-1]

    @pl.when(pl.program_id(1) == 0)
    def _():
        wbf[...] = w_ref[...].astype(BF16)

    acc, off = None, 0
    for lr in lhs:
        kk = lr.shape[1]
        if w_is_t:
            part = lax.dot_general(lr[...], wbf[:, off:off + kk], (((1,), (1,)), ((), ())),
                                   preferred_element_type=F32)
        else:
            part = jnp.dot(lr[...], wbf[off:off + kk, :], preferred_element_type=F32)
        acc = part if acc is None else acc + part
        off += kk
    if has_res:
        acc = alpha * r_ref[...] + acc
    o_ref[...] = acc.astype(o_ref.dtype)


def _matmul_wcast(lhs_list, w, layer, n_cols, tm, tn, out_dtype, res=None, alpha=1.0, w_is_t=False,
                  name="matmul"):
    m = lhs_list[0].shape[0]
    k = w.shape[-1] if w_is_t else w.shape[-2]
    assert sum(a.shape[1] for a in lhs_list) == k
    tm = min(tm, m)
    tn = max(t for t in range(LANES, min(tn, n_cols) + 1, LANES) if n_cols % t == 0)
    assert m % tm == 0
    in_specs = [pl.BlockSpec((tm, a.shape[1]), lambda j, i: (i, 0)) for a in lhs_list]
    blk = (tn, k) if w_is_t else (k, tn)
    if w.ndim == 3:
        w_map = (lambda j, i: (layer, j, 0)) if w_is_t else (lambda j, i: (layer, 0, j))
        in_specs.append(pl.BlockSpec((None,) + blk, w_map))
    else:
        in_specs.append(pl.BlockSpec(blk, (lambda j, i: (j, 0)) if w_is_t else (lambda j, i: (0, j))))
    args = list(lhs_list) + [w]
    if res is not None:
        in_specs.append(pl.BlockSpec((tm, tn), lambda j, i: (i, j)))
        args.append(res)
    return pl.pallas_call(
        functools.partial(_mm_wcast_kernel, len(lhs_list), alpha, res is not None, w_is_t),
        grid=(n_cols // tn, m // tm), in_specs=in_specs,
        out_specs=pl.BlockSpec((tm, tn), lambda j, i: (i, j)),
        out_shape=jax.ShapeDtypeStruct((m, n_cols), out_dtype),
        scratch_shapes=[pltpu.VMEM(blk, BF16)],
        compiler_params=_cparams(("arbitrary", "arbitrary")), name=name)(*args)


def _dn_kernel(q_ref, k_ref, v_ref, z_ref, ba_ref, cwq_ref, cwk_ref, cwv_ref, alog_ref, dtb_ref,
               nw_ref, o_ref, qbuf, kbuf, vbuf, s_ref):
    c_len, d = CHUNK, DN_HEAD_DIM
    n_grp = s_ref.shape[0]
    gw = n_grp * d
    hg = pl.program_id(1)
    c = pl.program_id(2)
    halo = SUBLANES

    @pl.when(c == 0)
    def _():
        s_ref[...] = jnp.zeros_like(s_ref)
        for buf in (qbuf, kbuf, vbuf):
            buf[0:halo, :] = jnp.zeros((halo, gw), F32)

    @pl.when(c > 0)
    def _():
        for buf in (qbuf, kbuf, vbuf):
            buf[0:halo, :] = buf[c_len:c_len + halo, :]

    def conv_silu(buf, x_ref, cw_ref):
        buf[halo:halo + c_len, :] = x_ref[...]
        acc = jnp.zeros((c_len, gw), F32)
        for j in range(DN_CONV):
            off = halo - (DN_CONV - 1) + j
            acc = acc + cw_ref[j:j + 1, :] * buf[off:off + c_len, :]
        return _silu(acc)

    q_all = conv_silu(qbuf, q_ref, cwq_ref)
    k_all = conv_silu(kbuf, k_ref, cwk_ref)
    v_all = conv_silu(vbuf, v_ref, cwv_ref)
    gate_all = nw_ref[...] * _silu(z_ref[...])

    ba = ba_ref[...]
    nh = pl.num_programs(1) * n_grp
    lane = _iota2((c_len, LANES), 1)
    row64 = _iota2((c_len, c_len), 0)
    col64 = _iota2((c_len, c_len), 1)
    tri_incl = (row64 >= col64).astype(BF16)
    causal = row64 >= col64
    strict = row64 > col64
    g_all = -jnp.exp(alog_ref[...]) * _softplus(ba + dtb_ref[...])
    gc_all = _dot_split(g_all, tri_incl, 3, ones_left=True)

    grp = list(range(n_grp))
    sls = [slice(gi * d, (gi + 1) * d) for gi in grp]

    def l2n(x):
        return x * lax.rsqrt(jnp.sum(x * x, axis=1, keepdims=True) + 1e-6)

    def pick(mat, lane_idx):
        return jnp.sum(jnp.where(lane == lane_idx, mat, 0.0), axis=1, keepdims=True)

    def decay_of(gc):
        gc_col = jnp.broadcast_to(gc, (c_len, c_len))
        gc_row = jnp.sum(jnp.where(row64 == col64, gc_col, 0.0), axis=0, keepdims=True)
        return jnp.exp(jnp.where(causal, gc_col - gc_row, NEG_BIG))

    q = [l2n(q_all[:, sl]) * (d ** -0.5) for sl in sls]
    k = [l2n(k_all[:, sl]) for sl in sls]
    v = [v_all[:, sl] for sl in sls]
    beta = [_sigmoid(pick(ba, hg * n_grp + gi)) for gi in grp]
    gc = [pick(gc_all, nh + hg * n_grp + gi) for gi in grp]
    gc_last = [x[c_len - 1:c_len, :] for x in gc]
    decay = _each(decay_of, gc)
    eg = _each(jnp.exp, gc)
    kb = _each(lambda a, b: a * b, k, beta)
    kkt = _each(_dot_nt, kb, k)
    low = _each(lambda m, dc: jnp.where(strict, m * dc, 0.0), kkt, decay)
    t_inv = _unit_lower_inverse(low, c_len, c_len)
    u = _each(lambda t, a, b: _dot(t, a * b), t_inv, v, beta)
    w = _each(lambda t, a, e: _dot(t, a * e), t_inv, kb, eg)
    qk = _each(lambda a, b, dc: _dot_nt(a, b) * dc, q, k, decay)
    s = [s_ref[gi] for gi in grp]
    ws = _each(_dot, w, s)
    v_new = _each(lambda a, b: a - b, u, ws)
    qs = _each(lambda a, e, st: _dot(a * e, st), q, eg, s)
    intra = _each(_dot, qk, v_new)
    kv = _each(lambda a, gl, g1, vn: _dot_tn(a * jnp.exp(gl - g1), vn), k, gc_last, gc, v_new)
    for gi in grp:
        s_ref[gi] = s[gi] * jnp.exp(gc_last[gi]) + kv[gi]
        out = qs[gi] + intra[gi]
        out = out * lax.rsqrt(jnp.mean(out * out, axis=1, keepdims=True) + DN_NORM_EPS)
        o_ref[:, sls[gi]] = (out * gate_all[:, sls[gi]]).astype(o_ref.dtype)


def _gated_deltanet(proj, tail, ba_blk, conv_w, alog_pad, dtb_pad, norm_w, bsz, seq, n_heads, n_grp):
    t_tot = bsz * seq
    nc = seq // CHUNK
    d = DN_HEAD_DIM
    assert n_heads % n_grp == 0
    n_hg = n_heads // n_grp
    gw = n_grp * d

    def col(group):
        return pl.BlockSpec((CHUNK, gw), lambda b, h, c, g=group: (b * nc + c, g * n_hg + h))

    def cw(group):
        return pl.BlockSpec((DN_CONV, gw), lambda b, h, c, g=group: (0, g * n_hg + h))

    vec = pl.BlockSpec((1, LANES), lambda b, h, c: (0, 0))
    return pl.pallas_call(
        _dn_kernel, grid=(bsz, n_hg, nc),
        in_specs=[col(0), col(1), col(2), col(3),
                  pl.BlockSpec((CHUNK, LANES), lambda b, h, c: (b * nc + c, ba_blk)),
                  cw(0), cw(1), cw(2), vec, vec,
                  pl.BlockSpec((1, gw), lambda b, h, c: (0, 0))],
        out_specs=pl.BlockSpec((CHUNK, gw), lambda b, h, c: (b * nc + c, h)),
        out_shape=jax.ShapeDtypeStruct((t_tot, n_heads * d), BF16),
        scratch_shapes=[pltpu.VMEM((CHUNK + SUBLANES, gw), F32)] * 3 + [pltpu.VMEM((n_grp, d, d), F32)],
        compiler_params=_cparams(("arbitrary", "arbitrary", "arbitrary")), name="gated_deltanet",
    )(proj, proj, proj, proj, tail, conv_w, conv_w, conv_w, alog_pad, dtb_pad, jnp.tile(norm_w, (1, n_grp)))


def _rw_kernel(has_vres, *refs):
    if has_vres:
        (r_ref, k_ref, v_ref, lo_ref, vf_ref, mur_ref, muk_ref, muv_ref, mulo_ref, w0_ref, a0_ref,
         kk_ref, ka_ref, rk_ref, gw_ref, gb_ref, w2_ref, a2_ref, g2_ref, v0_ref, v2_ref,
         o_ref, rbuf, kbuf, vbuf, lobuf, ht_ref) = refs
    else:
        (r_ref, k_ref, v_ref, lo_ref, mur_ref, muk_ref, muv_ref, mulo_ref, w0_ref, a0_ref,
         kk_ref, ka_ref, rk_ref, gw_ref, gb_ref, w2_ref, a2_ref, g2_ref,
         o_ref, vf_out_ref, rbuf, kbuf, vbuf, lobuf, ht_ref) = refs
    c_len, n = CHUNK, RW_HEAD_DIM
    n_grp = ht_ref.shape[0]
    c = pl.program_id(2)
    halo = SUBLANES
    bufs = (rbuf, kbuf, vbuf, lobuf)

    @pl.when(c == 0)
    def _():
        ht_ref[...] = jnp.zeros_like(ht_ref)
        for buf in bufs:
            buf[0:halo, :] = jnp.zeros((halo, buf.shape[1]), F32)

    @pl.when(c > 0)
    def _():
        for buf in bufs:
            buf[0:halo, :] = buf[c_len:c_len + halo, :]

    def shift_mix(buf, x_ref, mu_ref):
        x = x_ref[...]
        buf[halo:halo + c_len, :] = x
        prev = buf[halo - 1:halo - 1 + c_len, :]
        return x + (prev - x) * mu_ref[...]

    r_all = shift_mix(rbuf, r_ref, mur_ref)
    k_all = shift_mix(kbuf, k_ref, muk_ref)
    v_all = shift_mix(vbuf, v_ref, muv_ref)
    lo = shift_mix(lobuf, lo_ref, mulo_ref)
    w_pre = w0_ref[...] + _dot(jnp.tanh(lo[:, 0:128]), w2_ref[...])
    w_log = -_softplus(-w_pre) - 0.5
    logd_all = -jnp.exp(w_log)
    a_all = _sigmoid(a0_ref[...] + _dot(lo[:, 128:256], a2_ref[...]))
    g_all = _dot(_sigmoid(lo[:, 256:512]), g2_ref[...])
    if has_vres:
        v_all = v_all + (vf_ref[...] - v_all) * _sigmoid(v0_ref[...] + _dot(lo[:, 512:640], v2_ref[...]))
    else:
        vf_out_ref[...] = v_all
    kk_all = k_all * kk_ref[...]
    k_all = k_all * (1.0 + (a_all - 1.0) * ka_ref[...])
    rk_all = r_all * k_all * rk_ref[...]

    row64 = _iota2((c_len, c_len), 0)
    col64 = _iota2((c_len, c_len), 1)
    tri_incl = (row64 >= col64).astype(BF16)
    gcum_all = _dot_split(logd_all, tri_incl, 3, ones_left=True)
    p_incl_all = jnp.exp(gcum_all)
    p_prev_all = jnp.exp(gcum_all - logd_all)
    p_inv_all = jnp.exp(-gcum_all)

    r128 = _iota2((LANES, LANES), 0)
    c128 = _iota2((LANES, LANES), 1)
    same_head = (r128 >> 6) == (c128 >> 6)
    seg_ones = same_head.astype(BF16)
    strict = r128 > c128
    incl = r128 >= c128
    head0 = _iota2((c_len, LANES), 1) < n

    def seg_sum(x):
        return _dot_split(x, seg_ones, 2)

    def stack(x):
        return jnp.concatenate([jnp.where(head0, x, 0.0), jnp.where(head0, 0.0, x)], axis=0)

    grp = list(range(n_grp))
    sls = [slice(gi * LANES, (gi + 1) * LANES) for gi in grp]

    def cut(x):
        return [x[:, sl] for sl in sls]

    r, k, v, a = cut(r_all), cut(k_all), cut(v_all), cut(a_all)
    p_incl, p_prev, p_inv = cut(p_incl_all), cut(p_prev_all), cut(p_inv_all)
    kk = cut(kk_all)
    kk_ss = _each(lambda x: seg_sum(x * x), kk)
    kk = _each(lambda x, ss: x * lax.rsqrt(ss + 1e-12), kk, kk_ss)
    a_s = _each(lambda x, p: stack(-x * p), kk, p_prev)
    r_s = _each(lambda x, p: stack(x * p), r, p_incl)
    b_s = _each(lambda x, y, p: stack(x * y * p), kk, a, p_inv)
    k_s = _each(lambda x, p: stack(x * p), k, p_inv)
    v_s = _each(stack, v)

    a_ab = _each(_dot_nt, a_s, b_s)
    a_ak = _each(_dot_nt, a_s, k_s)
    a_rb = _each(_dot_nt, r_s, b_s)
    a_rk = _each(_dot_nt, r_s, k_s)
    t_inv = _unit_lower_inverse(_each(lambda m: jnp.where(strict, -m, 0.0), a_ab), 2 * c_len, c_len)
    akv = _each(lambda m, x: _dot(jnp.where(strict, m, 0.0), x), a_ak, v_s)
    u0 = _each(_dot, t_inv, akv)
    w = _each(_dot, t_inv, a_s)

    ht = [ht_ref[gi] for gi in grp]
    wh = _each(_dot_nt, w, ht)
    u = _each(lambda x, y: x + y, wh, u0)
    y_h = _each(_dot_nt, r_s, ht)
    y_u = _each(lambda m, x: _dot(jnp.where(incl, m, 0.0), x), a_rb, u)
    y_v = _each(lambda m, x: _dot(jnp.where(incl, m, 0.0), x), a_rk, v_s)
    h_u = _each(_dot_tn, u, b_s)
    h_v = _each(_dot_tn, v_s, k_s)
    y = []
    for gi in grp:
        p_last = p_incl[gi][c_len - 1:c_len, :]
        ht_ref[gi] = jnp.where(same_head, (ht[gi] + h_u[gi] + h_v[gi]) * p_last, 0.0)
        y_s = y_h[gi] + y_u[gi] + y_v[gi]
        y.append(y_s[0:c_len, :] + y_s[c_len:2 * c_len, :])

    mu = _each(lambda x: seg_sum(x) * (1.0 / n), y)
    yc = _each(lambda x, m: x - m, y, mu)
    var = _each(lambda x: seg_sum(x * x) * (1.0 / n), yc)
    bonus = _each(seg_sum, cut(rk_all))
    for gi in grp:
        sl = sls[gi]
        yn = yc[gi] * lax.rsqrt(var[gi] + RW_GN_EPS) * gw_ref[:, sl] + gb_ref[:, sl]
        o_ref[:, sl] = ((yn + bonus[gi] * v[gi]) * g_all[:, sl]).astype(o_ref.dtype)


def _rwkv7(proj, tail, r_blk0, v_first, prm, bsz, seq, n_pairs, n_grp):
    t_tot = bsz * seq
    nc = seq // CHUNK
    w_rw = n_pairs * LANES
    gw = n_grp * LANES
    assert n_pairs % n_grp == 0 and r_blk0 % n_grp == 0
    n_pg = n_pairs // n_grp
    has_vres = v_first is not None

    def col(group):
        return pl.BlockSpec((CHUNK, gw), lambda b, p, c, g=group: (b * nc + c, r_blk0 // n_grp + g * n_pg + p))

    tok = pl.BlockSpec((CHUNK, gw), lambda b, p, c: (b * nc + c, p))
    lo_spec = pl.BlockSpec((CHUNK, 5 * LANES), lambda b, p, c: (b * nc + c, 0))
    vec = pl.BlockSpec((1, gw), lambda b, p, c: (0, p))
    vec_lo = pl.BlockSpec((1, 5 * LANES), lambda b, p, c: (0, 0))

    def lowrank(rank):
        return pl.BlockSpec((rank, gw), lambda b, p, c: (0, p))

    in_specs = [col(0), col(1), col(2), lo_spec]
    args = [proj, proj, proj, tail]
    if has_vres:
        in_specs.append(tok)
        args.append(v_first)
    in_specs += [vec, vec, vec, vec_lo] + [vec] * 7 + [lowrank(128), lowrank(128), lowrank(256)]
    args += [prm["mu_r"], prm["mu_k"], prm["mu_v"], prm["mu_lo"], prm["w0"], prm["a0"], prm["k_k"],
             prm["k_a"], prm["r_k"], prm["gn_w"], prm["gn_b"], prm["w2"], prm["a2"], prm["g2"]]
    if has_vres:
        in_specs += [vec, lowrank(128)]
        args += [prm["v0"], prm["v2"]]
    out_shape = [jax.ShapeDtypeStruct((t_tot, w_rw), BF16)]
    out_specs = [tok]
    if not has_vres:
        out_shape.append(jax.ShapeDtypeStruct((t_tot, w_rw), F32))
        out_specs.append(tok)
    halo_rows = CHUNK + SUBLANES
    outs = pl.pallas_call(
        functools.partial(_rw_kernel, has_vres), grid=(bsz, n_pg, nc),
        in_specs=in_specs, out_specs=out_specs, out_shape=out_shape,
        scratch_shapes=[pltpu.VMEM((halo_rows, gw), F32)] * 3
        + [pltpu.VMEM((halo_rows, 5 * LANES), F32), pltpu.VMEM((n_grp, LANES, LANES), F32)],
        compiler_params=_cparams(("arbitrary", "arbitrary", "arbitrary")), name="rwkv7",
    )(*args)
    if has_vres:
        return outs[0], v_first
    return outs[0], outs[1]


def _layer_norm_rows(y, g, b):
    mu = jnp.mean(y, axis=1, keepdims=True)
    yc = y - mu
    var = jnp.mean(yc * yc, axis=1, keepdims=True)
    return yc * lax.rsqrt(var + LN_EPS) * g + b


def _ln_router_kernel(y_ref, g_ref, b_ref, rw_ref, rb_ref, x_ref, ti_ref, tw_ref, pos_ref, cnt_ref,
                      base_ref):
    tm = y_ref.shape[0]
    i = pl.program_id(0)

    @pl.when(i == 0)
    def _():
        base_ref[...] = jnp.zeros_like(base_ref)

    x = _layer_norm_rows(y_ref[...], g_ref[...], b_ref[...])
    x_ref[...] = x
    logits = _dot(x, rw_ref[...]) + rb_ref[...]
    lane = _iota2((tm, LANES), 1)
    lane_f = lane.astype(F32)
    work = logits
    sels, vals, idxs = [], [], []
    for _ in range(TOP_K):
        m = jnp.max(work, axis=1, keepdims=True)
        idx = jnp.min(jnp.where(work == m, lane_f, float(LANES)), axis=1, keepdims=True)
        sel = lane_f == idx
        sels.append(sel)
        vals.append(m)
        idxs.append(idx)
        work = jnp.where(sel, 2.0 * NEG_BIG, work)
    exps = [jnp.exp(m - vals[0]) for m in vals]
    denom = exps[0] + exps[1] + exps[2] + exps[3]
    cnt = jnp.zeros((tm, LANES), F32)
    for sel in sels:
        cnt = cnt + sel.astype(F32)
    rowt = _iota2((tm, tm), 0)
    colt = _iota2((tm, tm), 1)
    before = _dot((rowt > colt).astype(F32), cnt)
    posmat = base_ref[...] + before
    ti = jnp.zeros((tm, LANES), F32)
    tw = jnp.zeros((tm, LANES), F32)
    pos = jnp.zeros((tm, LANES), F32)
    for kk in range(TOP_K):
        here = lane == kk
        ti = jnp.where(here, idxs[kk], ti)
        tw = jnp.where(here, exps[kk] / denom, tw)
        pos = jnp.where(here, jnp.sum(jnp.where(sels[kk], posmat, 0.0), axis=1, keepdims=True), pos)
    ti_ref[...] = ti.astype(I32)
    tw_ref[...] = tw
    pos_ref[...] = pos.astype(I32)
    base_new = base_ref[...] + jnp.sum(cnt, axis=0, keepdims=True)
    base_ref[...] = base_new
    cnt_ref[...] = jnp.broadcast_to(base_new, cnt_ref.shape).astype(I32)


def _ln_router(y, g, b, rw_pad, rb_pad, tm):
    t_tot, dm = y.shape
    tm = min(tm, t_tot)
    assert t_tot % tm == 0
    row = pl.BlockSpec((tm, dm), lambda i: (i, 0))
    vec = pl.BlockSpec((1, dm), lambda i: (0, 0))
    small = pl.BlockSpec((tm, LANES), lambda i: (i, 0))
    return pl.pallas_call(
        _ln_router_kernel, grid=(t_tot // tm,),
        in_specs=[row, vec, vec, pl.BlockSpec((dm, LANES), lambda i: (0, 0)),
                  pl.BlockSpec((1, LANES), lambda i: (0, 0))],
        out_specs=[row, small, small, small, pl.BlockSpec((SUBLANES, LANES), lambda i: (0, 0))],
        out_shape=[jax.ShapeDtypeStruct((t_tot, dm), F32),
                   jax.ShapeDtypeStruct((t_tot, LANES), I32),
                   jax.ShapeDtypeStruct((t_tot, LANES), F32),
                   jax.ShapeDtypeStruct((t_tot, LANES), I32),
                   jax.ShapeDtypeStruct((SUBLANES, LANES), I32)],
        scratch_shapes=[pltpu.VMEM((1, LANES), F32)],
        compiler_params=_cparams(("arbitrary",)), name="ln_router",
    )(y, g, b, rw_pad, rb_pad)


def _row_gather_start(src3, dst3, sem, idx_of_row, n_rows):
    def body(g, carry):
        for j in range(SUBLANES):
            idx = idx_of_row(g * SUBLANES + j)
            pltpu.make_async_copy(src3.at[idx >> 3, pl.ds(idx & (SUBLANES - 1), 1)],
                                  dst3.at[g, pl.ds(j, 1)], sem).start()
        return carry
    assert n_rows % SUBLANES == 0
    lax.fori_loop(0, n_rows // SUBLANES, body, 0)


def _row_gather_wait(dst3, sem):
    pltpu.make_async_copy(dst3, dst3, sem).wait()


def _moe_up_kernel(te_ref, nu_ref, tos_ref, x_hbm, wg_ref, wu_ref, bg_ref, bu_ref, h_ref, xbuf, sem):
    tm = h_ref.shape[0]
    i = pl.program_id(0)
    slot = lax.rem(i, 2)
    n_used = nu_ref[0]

    def start(tile, s):
        _row_gather_start(x_hbm, xbuf.at[s], sem.at[s], lambda r: tos_ref[tile * tm + r], tm)

    @pl.when(i == 0)
    def _():
        start(0, 0)

    @pl.when(i + 1 < n_used)
    def _():
        start(i + 1, 1 - slot)

    @pl.when(i < n_used)
    def _():
        _row_gather_wait(xbuf.at[slot], sem.at[slot])
        x = xbuf[slot].reshape(tm, xbuf.shape[-1]).astype(BF16)
        gate = jnp.dot(x, wg_ref[...], preferred_element_type=F32) + bg_ref[...]
        up = jnp.dot(x, wu_ref[...], preferred_element_type=F32) + bu_ref[...]
        gate = jnp.minimum(gate, SWIGLU_LIMIT)
        up = jnp.clip(up, -SWIGLU_LIMIT, SWIGLU_LIMIT)
        h_ref[...] = ((up + 1.0) * gate * _sigmoid(gate * SWIGLU_ALPHA)).astype(h_ref.dtype)

    @pl.when(i >= n_used)
    def _():
        h_ref[...] = jnp.zeros_like(h_ref)


def _moe_down_kernel(te_ref, nu_ref, h_ref, wd_ref, bd_ref, y_ref, wbf):
    i = pl.program_id(0)
    new_expert = jnp.logical_or(i == 0, te_ref[i] != te_ref[jnp.maximum(i - 1, 0)])

    @pl.when(jnp.logical_and(new_expert, i < nu_ref[0]))
    def _():
        wbf[...] = wd_ref[...].astype(BF16)

    @pl.when(i < nu_ref[0])
    def _():
        y_ref[...] = (jnp.dot(h_ref[...], wbf[...], preferred_element_type=F32) + bd_ref[...]).astype(y_ref.dtype)

    @pl.when(i >= nu_ref[0])
    def _():
        y_ref[...] = jnp.zeros_like(y_ref)


def _moe_experts(x, tile_expert, n_used, tok_of_slot, wg, wu, bg, bu, wd_all, layer, bd, tm):
    n_slots = tok_of_slot.shape[0]
    n_tiles = n_slots // tm
    _, n_exp, dm, de = wg.shape
    h = pl.pallas_call(
        _moe_up_kernel,
        grid_spec=pltpu.PrefetchScalarGridSpec(
            num_scalar_prefetch=3, grid=(n_tiles,),
            in_specs=[pl.BlockSpec(memory_space=pl.ANY),
                      pl.BlockSpec((None, None, dm, de), lambda i, te, nu, tos: (layer, te[i], 0, 0)),
                      pl.BlockSpec((None, None, dm, de), lambda i, te, nu, tos: (layer, te[i], 0, 0)),
                      pl.BlockSpec((None, 1, de), lambda i, te, nu, tos: (te[i], 0, 0)),
                      pl.BlockSpec((None, 1, de), lambda i, te, nu, tos: (te[i], 0, 0))],
            out_specs=pl.BlockSpec((tm, de), lambda i, te, nu, tos: (i, 0)),
            scratch_shapes=[pltpu.VMEM((2, tm // SUBLANES, SUBLANES, dm), F32),
                            pltpu.SemaphoreType.DMA((2,))]),
        out_shape=jax.ShapeDtypeStruct((n_slots, de), BF16),
        compiler_params=_cparams(("arbitrary",)), name="moe_gate_up",
    )(tile_expert, n_used, tok_of_slot, x.reshape(-1, SUBLANES, dm), wg, wu, bg, bu)
    return pl.pallas_call(
        _moe_down_kernel,
        grid_spec=pltpu.PrefetchScalarGridSpec(
            num_scalar_prefetch=2, grid=(n_tiles,),
            in_specs=[pl.BlockSpec((tm, de), lambda i, te, nu: (i, 0)),
                      pl.BlockSpec((None, None, de, dm), lambda i, te, nu: (layer, te[i], 0, 0)),
                      pl.BlockSpec((None, 1, dm), lambda i, te, nu: (te[i], 0, 0))],
            out_specs=pl.BlockSpec((tm, dm), lambda i, te, nu: (i, 0)),
            scratch_shapes=[pltpu.VMEM((de, dm), BF16)]),
        out_shape=jax.ShapeDtypeStruct((n_slots, dm), F32),
        compiler_params=_cparams(("arbitrary",)), name="moe_down",
    )(tile_expert, n_used, h, wd_all, bd)


def _combine_ln_kernel(alpha, dest_ref, ys_hbm, x_ref, tw_ref, g_ref, b_ref, o_ref, ob_ref, ybuf, sem):
    tm = x_ref.shape[0]
    i = pl.program_id(0)
    n = pl.num_programs(0)
    slot = lax.rem(i, 2)

    def start(tile, s):
        for kk in range(TOP_K):
            _row_gather_start(ys_hbm, ybuf.at[s, kk], sem.at[s],
                              lambda r, kk=kk: dest_ref[(tile * tm + r) * TOP_K + kk], tm)

    @pl.when(i == 0)
    def _():
        start(0, 0)

    @pl.when(i + 1 < n)
    def _():
        start(i + 1, 1 - slot)

    for kk in range(TOP_K):
        _row_gather_wait(ybuf.at[slot, kk], sem.at[slot])
    tw = tw_ref[...]
    acc = alpha * x_ref[...]
    for kk in range(TOP_K):
        acc = acc + tw[:, kk:kk + 1] * ybuf[slot, kk].reshape(tm, ybuf.shape[-1])
    out = _layer_norm_rows(acc, g_ref[...], b_ref[...])
    o_ref[...] = out
    ob_ref[...] = out.astype(BF16)


def _combine_ln(ys, x, tw, dest_flat, g, b, alpha, tm):
    t_tot, dm = x.shape
    tm = min(tm, t_tot)
    assert t_tot % tm == 0
    row = pl.BlockSpec((tm, dm), lambda i, d: (i, 0))
    vec = pl.BlockSpec((1, dm), lambda i, d: (0, 0))
    return pl.pallas_call(
        functools.partial(_combine_ln_kernel, alpha),
        grid_spec=pltpu.PrefetchScalarGridSpec(
            num_scalar_prefetch=1, grid=(t_tot // tm,),
            in_specs=[pl.BlockSpec(memory_space=pl.ANY), row,
                      pl.BlockSpec((tm, LANES), lambda i, d: (i, 0)), vec, vec],
            out_specs=[row, row],
            scratch_shapes=[pltpu.VMEM((2, TOP_K, tm // SUBLANES, SUBLANES, dm), F32),
                            pltpu.SemaphoreType.DMA((2,))]),
        out_shape=[jax.ShapeDtypeStruct((t_tot, dm), F32), jax.ShapeDtypeStruct((t_tot, dm), BF16)],
        compiler_params=_cparams(("arbitrary",)), name="moe_combine_ln",
    )(dest_flat, ys.reshape(-1, SUBLANES, dm), x, tw, g, b)


def _tiles(t_tot):
    return dict(mm_m=min(1024, t_tot), mm_n=1024, mmw_n=512, ln_m=min(256, t_tot), moe_m=min(512, t_tot),
                comb_m=min(64, t_tot), rec_grp=16)


def _pad_cols(w, width):
    return jnp.pad(w, ((0, 0), (0, width - w.shape[1])))


def _pad_rows(w, rows):
    return jnp.pad(w, ((0, rows - w.shape[0]), (0, 0)))


def kernel(x, mix_w_in, mix_w_in_vres, dn_conv_w, dn_a_log, dn_dt_bias, dn_norm_w, rw_shift_mu, rw_shift_mu_vres, rw_w0, rw_w2, rw_a0, rw_a2, rw_g2, rw_v0, rw_v2, rw_k_k, rw_k_a, rw_r_k, rw_gn_w, rw_gn_b, mix_w_out, ln1_g, ln1_b, router_w, router_b, exp_w_gate, exp_b_gate, exp_w_up, exp_b_up, exp_w_down, exp_b_down, ln2_g, ln2_b):
    bsz, seq, dm = x.shape
    depth = mix_w_in.shape[0]
    t_tot = bsz * seq
    dn_heads = dn_a_log.shape[1]
    w_dn = dn_heads * DN_HEAD_DIM
    w_rw = rw_w0.shape[1]
    n_pairs = w_rw // LANES
    r_dec, r_aaa, r_gate, r_mv = rw_w2.shape[1], rw_a2.shape[1], rw_g2.shape[1], rw_v2.shape[1]
    n_exp = router_w.shape[2]
    assert seq % CHUNK == 0 and w_rw % LANES == 0 and 2 * dn_heads <= LANES
    assert max(r_dec, r_aaa, r_mv) <= LANES and r_gate == 2 * LANES and n_exp <= LANES
    alpha = (2 * depth) ** 0.25
    tl = _tiles(t_tot)
    n_dn_in = 4 * w_dn + 2 * dn_heads

    x2d = x.reshape(t_tot, dm)
    xb = x2d.astype(BF16)
    w_in_t = jnp.swapaxes(mix_w_in, 1, 2)
    w_vres_t = jnp.swapaxes(mix_w_in_vres, 1, 2)
    wg_bf, wu_bf = exp_w_gate.astype(BF16), exp_w_up.astype(BF16)
    v_first = None
    for l in range(depth):
        rw0 = n_dn_in
        o_lo = rw0 + 3 * w_rw
        o_a, o_g = o_lo + r_dec, o_lo + r_dec + r_aaa
        v_lo_t = w_vres_t[l - 1] if l > 0 else jnp.zeros((r_mv, dm), F32)
        w_tail_t = jnp.concatenate([
            _pad_rows(w_in_t[l, o_lo:o_a], LANES), _pad_rows(w_in_t[l, o_a:o_g], LANES), w_in_t[l, o_g:o_g + r_gate],
            _pad_rows(v_lo_t, LANES), _pad_rows(w_in_t[l, 4 * w_dn:n_dn_in], LANES)], axis=0)
        proj_dn = _matmul_wcast([xb], w_in_t, l, 4 * w_dn, tl["mm_m"], tl["mmw_n"], F32, w_is_t=True,
                                name="in_proj_dn")
        proj_rw = _matmul_wcast([xb], w_in_t[l, rw0:o_lo], 0, 3 * w_rw, tl["mm_m"], tl["mmw_n"], F32, w_is_t=True,
                                name="in_proj_rw")
        tail = _matmul_wcast([xb], w_tail_t, 0, w_tail_t.shape[0], tl["mm_m"], w_tail_t.shape[0], F32, w_is_t=True,
                             name="in_proj_tail")

        zeros_h = jnp.zeros((dn_heads,), F32)
        alog_pad = _pad_cols(jnp.concatenate([zeros_h, dn_a_log[l]])[None, :], LANES)
        dtb_pad = _pad_cols(jnp.concatenate([zeros_h, dn_dt_bias[l]])[None, :], LANES)
        o_dn = _gated_deltanet(proj_dn, tail, 5, dn_conv_w[l], alog_pad, dtb_pad, dn_norm_w[l][None, :],
                               bsz, seq, dn_heads, min(tl["rec_grp"], dn_heads))

        mu = rw_shift_mu[l]
        mu_lo_parts = [_pad_cols(mu[None, 3 * w_rw:3 * w_rw + r_dec], LANES),
                       _pad_cols(mu[None, 3 * w_rw + r_dec:3 * w_rw + r_dec + r_aaa], LANES),
                       mu[None, 3 * w_rw + r_dec + r_aaa:],
                       _pad_cols(rw_shift_mu_vres[l - 1][None, :] if l > 0 else jnp.zeros((1, r_mv), F32), LANES)]
        prm = dict(mu_r=mu[None, :w_rw], mu_k=mu[None, w_rw:2 * w_rw], mu_v=mu[None, 2 * w_rw:3 * w_rw],
                   mu_lo=jnp.concatenate(mu_lo_parts, axis=1),
                   w0=rw_w0[l][None, :], a0=rw_a0[l][None, :], k_k=rw_k_k[l][None, :], k_a=rw_k_a[l][None, :],
                   r_k=rw_r_k[l].reshape(1, w_rw), gn_w=rw_gn_w[l][None, :], gn_b=rw_gn_b[l][None, :],
                   w2=_pad_rows(rw_w2[l], LANES), a2=_pad_rows(rw_a2[l], LANES), g2=rw_g2[l])
        if l > 0:
            prm["v0"] = rw_v0[l - 1][None, :]
            prm["v2"] = _pad_rows(rw_v2[l - 1], LANES)
        o_rw, v_first = _rwkv7(proj_rw, tail, 0, v_first if l > 0 else None, prm, bsz, seq, n_pairs,
                               min(tl["rec_grp"], n_pairs))

        y1 = _matmul_wcast([o_dn, o_rw], mix_w_out, l, dm, tl["mm_m"], tl["mmw_n"], F32, res=x2d, alpha=alpha,
                           name="out_proj")
        rw_pad = _pad_cols(router_w[l], LANES)
        rb_pad = jnp.concatenate([router_b[l], jnp.full((LANES - n_exp,), NEG_BIG, F32)])[None, :]
        x1, top_i, top_w, pos, cnt = _ln_router(y1, ln1_g[l][None, :], ln1_b[l][None, :], rw_pad, rb_pad,
                                                tl["ln_m"])

        tm = tl["moe_m"]
        counts = cnt[0, :n_exp]
        padded = ((counts + tm - 1) // tm) * tm
        ends = jnp.cumsum(padded)
        offs = ends - padded
        n_tiles = (t_tot * TOP_K) // tm + n_exp
        dest = offs[top_i[:, :TOP_K]] + pos[:, :TOP_K]
        tok_ids = jnp.broadcast_to(jnp.arange(t_tot, dtype=I32)[:, None], (t_tot, TOP_K))
        tok_of_slot = jnp.zeros((n_tiles * tm,), I32).at[dest.reshape(-1)].set(tok_ids.reshape(-1))
        tile_start = jnp.arange(n_tiles, dtype=I32) * tm
        tile_expert = jnp.minimum(jnp.sum((ends[None, :] <= tile_start[:, None]).astype(I32), axis=1), n_exp - 1)
        n_used = (ends[-1] // tm).astype(I32)[None]

        ys = _moe_experts(x1, tile_expert, n_used, tok_of_slot,
                          wg_bf, wu_bf,
                          exp_b_gate[l][:, None, :], exp_b_up[l][:, None, :],
                          exp_w_down, l, exp_b_down[l][:, None, :], tm)
        x2d, xb = _combine_ln(ys, x1, top_w, dest.reshape(-1), ln2_g[l][None, :], ln2_b[l][None, :],
                              alpha, tl["comb_m"])
    return x2d.reshape(bsz, seq, dm)
```

```python
import functools
import math

import jax
import jax.numpy as jnp
from jax import lax
from jax.experimental import pallas as pl
from jax.experimental.pallas import tpu as pltpu

F32 = jnp.float32
BF16 = jnp.bfloat16
I32 = jnp.int32

LANES = 128
SUBLANES = 8
VMEM_LIMIT = 56 * 1024 * 1024

CHUNK = 64
DN_HEAD_DIM = 128
DN_CONV = 4
DN_NORM_EPS = 1e-6
RW_HEAD_DIM = 64
RW_GN_EPS = 64e-5
TOP_K = 4
SWIGLU_ALPHA = 1.702
SWIGLU_LIMIT = 7.0
LN_EPS = 1e-5
NEG_BIG = -1e30


def _dot(a, b):
    return jnp.dot(a.astype(BF16), b.astype(BF16), preferred_element_type=F32)


def _dot_nt(a, b):
    return lax.dot_general(a.astype(BF16), b.astype(BF16), (((1,), (1,)), ((), ())),
                           preferred_element_type=F32)


def _dot_tn(a, b):
    return lax.dot_general(a.astype(BF16), b.astype(BF16), (((0,), (0,)), ((), ())),
                           preferred_element_type=F32)


def _dot_split(a, ones, passes, ones_left=False):
    acc, rem = None, a
    for p in range(passes):
        term = rem.astype(BF16)
        part = (jnp.dot(ones, term, preferred_element_type=F32) if ones_left
                else jnp.dot(term, ones, preferred_element_type=F32))
        acc = part if acc is None else acc + part
        if p + 1 < passes:
            rem = rem - term.astype(F32)
    return acc


def _iota2(shape, dim):
    return lax.broadcasted_iota(I32, shape, dim)


def _sigmoid(x):
    return 1.0 / (1.0 + jnp.exp(-x))


def _softplus(x):
    return jnp.maximum(x, 0.0) + jnp.log(1.0 + jnp.exp(-jnp.abs(x)))


def _silu(x):
    return x * _sigmoid(x)


def _each(fn, *lists):
    return [fn(*args) for args in zip(*lists)]


def _unit_lower_inverse(lows, n, blk):
    row = _iota2((n, n), 0)
    col = _iota2((n, n), 1)
    eye = (row == col).astype(F32)
    pair = (row >> 1) == (col >> 1)
    invs = _each(lambda low: eye - jnp.where(pair, low, 0.0), lows)
    shift = 1
    while (1 << shift) < blk:
        same_big = (row >> (shift + 1)) == (col >> (shift + 1))
        diff_small = (row >> shift) != (col >> shift)
        mask = jnp.logical_and(same_big, diff_small)
        cms = _each(lambda low: jnp.where(mask, low, 0.0), lows)
        tmp = _each(_dot, invs, cms)
        upd = _each(_dot, tmp, invs)
        invs = _each(lambda inv, d: inv - d, invs, upd)
        shift += 1
    return invs


def _cparams(sem):
    return pltpu.CompilerParams(dimension_semantics=sem, vmem_limit_bytes=VMEM_LIMIT)


def _mm_kernel(x_ref, w_ref, o_ref):
    o_ref[...] = jnp.dot(x_ref[...], w_ref[...], preferred_element_type=F32).astype(o_ref.dtype)


def _mm_res_kernel(alpha, x_ref, w_ref, r_ref, o_ref):
    acc = jnp.dot(x_ref[...], w_ref[...], preferred_element_type=F32)
    o_ref[...] = (alpha * r_ref[...] + acc).astype(o_ref.dtype)


def _matmul(x, w, tm, tn, out_dtype, res=None, alpha=1.0, name="matmul"):
    m, k = x.shape
    n = w.shape[1]
    tm = min(tm, m)
    tn = max(t for t in range(LANES, min(tn, n) + 1, LANES) if n % t == 0)
    assert m % tm == 0 and n % LANES == 0
    in_specs = [pl.BlockSpec((tm, k), lambda i, j: (i, 0)),
                pl.BlockSpec((k, tn), lambda i, j: (0, j))]
    args = [x, w]
    body = _mm_kernel
    if res is not None:
        in_specs.append(pl.BlockSpec((tm, tn), lambda i, j: (i, j)))
        args.append(res)
        body = functools.partial(_mm_res_kernel, alpha)
    return pl.pallas_call(
        body, grid=(m // tm, n // tn), in_specs=in_specs,
        out_specs=pl.BlockSpec((tm, tn), lambda i, j: (i, j)),
        out_shape=jax.ShapeDtypeStruct((m, n), out_dtype),
        compiler_params=_cparams(("arbitrary", "arbitrary")), name=name)(*args)


def _mm_wcast_kernel(n_lhs, alpha, has_res, w_is_t, *refs):
    lhs = refs[:n_lhs]
    w_ref = refs[n_lhs]
    r_ref = refs[n_lhs + 1] if has_res else None
    o_ref = refs[n_lhs + 1 + int(has_res)]
    wbf = refs[-1]

    @pl.when(pl.program_id(1) == 0)
    def _():
        wbf[...] = w_ref[...].astype(BF16)

    acc, off = None, 0
    for lr in lhs:
        kk = lr.shape[1]
        if w_is_t:
            part = lax.dot_general(lr[...], wbf[:, off:off + kk], (((1,), (1,)), ((), ())),
                                   preferred_element_type=F32)
        else:
            part = jnp.dot(lr[...], wbf[off:off + kk, :], preferred_element_type=F32)
        acc = part if acc is None else acc + part
        off += kk
    if has_res:
        acc = alpha * r_ref[...] + acc
    o_ref[...] = acc.astype(o_ref.dtype)


def _matmul_wcast(lhs_list, w, layer, n_cols, tm, tn, out_dtype, res=None, alpha=1.0, w_is_t=False,
                  w_single_buffer=False, name="matmul"):
    m = lhs_list[0].shape[0]
    k = w.shape[-1] if w_is_t else w.shape[-2]
    assert sum(a.shape[1] for a in lhs_list) == k
    tm = min(tm, m)
    tn = max(t for t in range(LANES, min(tn, n_cols) + 1, LANES) if n_cols % t == 0)
    assert m % tm == 0
    in_specs = [pl.BlockSpec((tm, a.shape[1]), lambda j, i: (i, 0)) for a in lhs_list]
    blk = (tn, k) if w_is_t else (k, tn)
    w_kw = dict(pipeline_mode=pl.Buffered(1)) if w_single_buffer else {}
    if w.ndim == 3:
        w_map = (lambda j, i: (layer, j, 0)) if w_is_t else (lambda j, i: (layer, 0, j))
        in_specs.append(pl.BlockSpec((None,) + blk, w_map, **w_kw))
    else:
        in_specs.append(pl.BlockSpec(blk, (lambda j, i: (j, 0)) if w_is_t else (lambda j, i: (0, j)), **w_kw))
    args = list(lhs_list) + [w]
    if res is not None:
        in_specs.append(pl.BlockSpec((tm, tn), lambda j, i: (i, j)))
        args.append(res)
    return pl.pallas_call(
        functools.partial(_mm_wcast_kernel, len(lhs_list), alpha, res is not None, w_is_t),
        grid=(n_cols // tn, m // tm), in_specs=in_specs,
        out_specs=pl.BlockSpec((tm, tn), lambda j, i: (i, j)),
        out_shape=jax.ShapeDtypeStruct((m, n_cols), out_dtype),
        scratch_shapes=[pltpu.VMEM(blk, BF16)],
        compiler_params=_cparams(("arbitrary", "arbitrary")), name=name)(*args)


def _dn_kernel(q_ref, k_ref, v_ref, z_ref, ba_ref, cwq_ref, cwk_ref, cwv_ref, alog_ref, dtb_ref,
               nw_ref, o_ref, qbuf, kbuf, vbuf, s_ref):
    c_len, d = CHUNK, DN_HEAD_DIM
    n_grp = s_ref.shape[0]
    gw = n_grp * d
    hg = pl.program_id(1)
    c = pl.program_id(2)
    halo = SUBLANES

    @pl.when(c == 0)
    def _():
        s_ref[...] = jnp.zeros_like(s_ref)
        for buf in (qbuf, kbuf, vbuf):
            buf[0:halo, :] = jnp.zeros((halo, gw), F32)

    @pl.when(c > 0)
    def _():
        for buf in (qbuf, kbuf, vbuf):
            buf[0:halo, :] = buf[c_len:c_len + halo, :]

    def conv_silu(buf, x_ref, cw_ref):
        buf[halo:halo + c_len, :] = x_ref[...]
        acc = jnp.zeros((c_len, gw), F32)
        for j in range(DN_CONV):
            off = halo - (DN_CONV - 1) + j
            acc = acc + cw_ref[j:j + 1, :] * buf[off:off + c_len, :]
        return _silu(acc)

    q_all = conv_silu(qbuf, q_ref, cwq_ref)
    k_all = conv_silu(kbuf, k_ref, cwk_ref)
    v_all = conv_silu(vbuf, v_ref, cwv_ref)
    gate_all = nw_ref[...] * _silu(z_ref[...])

    ba = ba_ref[...]
    nh = pl.num_programs(1) * n_grp
    lane = _iota2((c_len, LANES), 1)
    row64 = _iota2((c_len, c_len), 0)
    col64 = _iota2((c_len, c_len), 1)
    tri_incl = (row64 >= col64).astype(BF16)
    causal = row64 >= col64
    strict = row64 > col64
    g_all = -jnp.exp(alog_ref[...]) * _softplus(ba + dtb_ref[...])
    gc_all = _dot_split(g_all, tri_incl, 3, ones_left=True)

    grp = list(range(n_grp))
    sls = [slice(gi * d, (gi + 1) * d) for gi in grp]

    def l2n(x):
        return x * lax.rsqrt(jnp.sum(x * x, axis=1, keepdims=True) + 1e-6)

    def pick(mat, lane_idx):
        return jnp.sum(jnp.where(lane == lane_idx, mat, 0.0), axis=1, keepdims=True)

    def decay_of(gc):
        gc_col = jnp.broadcast_to(gc, (c_len, c_len))
        gc_row = jnp.sum(jnp.where(row64 == col64, gc_col, 0.0), axis=0, keepdims=True)
        return jnp.exp(jnp.where(causal, gc_col - gc_row, NEG_BIG))

    q = [l2n(q_all[:, sl]) * (d ** -0.5) for sl in sls]
    k = [l2n(k_all[:, sl]) for sl in sls]
    v = [v_all[:, sl] for sl in sls]
    beta = [_sigmoid(pick(ba, hg * n_grp + gi)) for gi in grp]
    gc = [pick(gc_all, nh + hg * n_grp + gi) for gi in grp]
    gc_last = [x[c_len - 1:c_len, :] for x in gc]
    decay = _each(decay_of, gc)
    eg = _each(jnp.exp, gc)
    kb = _each(lambda a, b: a * b, k, beta)
    kkt = _each(_dot_nt, kb, k)
    low = _each(lambda m, dc: jnp.where(strict, m * dc, 0.0), kkt, decay)
    t_inv = _unit_lower_inverse(low, c_len, c_len)
    u = _each(lambda t, a, b: _dot(t, a * b), t_inv, v, beta)
    w = _each(lambda t, a, e: _dot(t, a * e), t_inv, kb, eg)
    qk = _each(lambda a, b, dc: _dot_nt(a, b) * dc, q, k, decay)
    s = [s_ref[gi] for gi in grp]
    ws = _each(_dot, w, s)
    v_new = _each(lambda a, b: a - b, u, ws)
    qs = _each(lambda a, e, st: _dot(a * e, st), q, eg, s)
    intra = _each(_dot, qk, v_new)
    kv = _each(lambda a, gl, g1, vn: _dot_tn(a * jnp.exp(gl - g1), vn), k, gc_last, gc, v_new)
    for gi in grp:
        s_ref[gi] = s[gi] * jnp.exp(gc_last[gi]) + kv[gi]
        out = qs[gi] + intra[gi]
        out = out * lax.rsqrt(jnp.mean(out * out, axis=1, keepdims=True) + DN_NORM_EPS)
        o_ref[:, sls[gi]] = (out * gate_all[:, sls[gi]]).astype(o_ref.dtype)


def _gated_deltanet(proj, tail, ba_blk, conv_w, alog_pad, dtb_pad, norm_w, bsz, seq, n_heads, n_grp):
    t_tot = bsz * seq
    nc = seq // CHUNK
    d = DN_HEAD_DIM
    assert n_heads % n_grp == 0
    n_hg = n_heads // n_grp
    gw = n_grp * d

    def col(group):
        return pl.BlockSpec((CHUNK, gw), lambda b, h, c, g=group: (b * nc + c, g * n_hg + h))

    def cw(group):
        return pl.BlockSpec((DN_CONV, gw), lambda b, h, c, g=group: (0, g * n_hg + h))

    vec = pl.BlockSpec((1, LANES), lambda b, h, c: (0, 0))
    return pl.pallas_call(
        _dn_kernel, grid=(bsz, n_hg, nc),
        in_specs=[col(0), col(1), col(2), col(3),
                  pl.BlockSpec((CHUNK, LANES), lambda b, h, c: (b * nc + c, ba_blk)),
                  cw(0), cw(1), cw(2), vec, vec,
                  pl.BlockSpec((1, gw), lambda b, h, c: (0, 0))],
        out_specs=pl.BlockSpec((CHUNK, gw), lambda b, h, c: (b * nc + c, h)),
        out_shape=jax.ShapeDtypeStruct((t_tot, n_heads * d), BF16),
        scratch_shapes=[pltpu.VMEM((CHUNK + SUBLANES, gw), F32)] * 3 + [pltpu.VMEM((n_grp, d, d), F32)],
        compiler_params=_cparams(("arbitrary", "arbitrary", "arbitrary")), name="gated_deltanet",
    )(proj, proj, proj, proj, tail, conv_w, conv_w, conv_w, alog_pad, dtb_pad, jnp.tile(norm_w, (1, n_grp)))


def _rw_kernel(has_vres, *refs):
    if has_vres:
        (r_ref, k_ref, v_ref, lo_ref, vf_ref, mur_ref, muk_ref, muv_ref, mulo_ref, w0_ref, a0_ref,
         kk_ref, ka_ref, rk_ref, gw_ref, gb_ref, w2_ref, a2_ref, g2_ref, v0_ref, v2_ref,
         o_ref, rbuf, kbuf, vbuf, lobuf, ht_ref) = refs
    else:
        (r_ref, k_ref, v_ref, lo_ref, mur_ref, muk_ref, muv_ref, mulo_ref, w0_ref, a0_ref,
         kk_ref, ka_ref, rk_ref, gw_ref, gb_ref, w2_ref, a2_ref, g2_ref,
         o_ref, vf_out_ref, rbuf, kbuf, vbuf, lobuf, ht_ref) = refs
    c_len, n = CHUNK, RW_HEAD_DIM
    n_grp = ht_ref.shape[0]
    c = pl.program_id(2)
    halo = SUBLANES
    bufs = (rbuf, kbuf, vbuf, lobuf)

    @pl.when(c == 0)
    def _():
        ht_ref[...] = jnp.zeros_like(ht_ref)
        for buf in bufs:
            buf[0:halo, :] = jnp.zeros((halo, buf.shape[1]), F32)

    @pl.when(c > 0)
    def _():
        for buf in bufs:
            buf[0:halo, :] = buf[c_len:c_len + halo, :]

    def shift_mix(buf, x_ref, mu_ref):
        x = x_ref[...]
        buf[halo:halo + c_len, :] = x
        prev = buf[halo - 1:halo - 1 + c_len, :]
        return x + (prev - x) * mu_ref[...]

    r_all = shift_mix(rbuf, r_ref, mur_ref)
    k_all = shift_mix(kbuf, k_ref, muk_ref)
    v_all = shift_mix(vbuf, v_ref, muv_ref)
    lo = shift_mix(lobuf, lo_ref, mulo_ref)
    w_pre = w0_ref[...] + _dot(jnp.tanh(lo[:, 0:128]), w2_ref[...])
    w_log = -_softplus(-w_pre) - 0.5
    logd_all = -jnp.exp(w_log)
    a_all = _sigmoid(a0_ref[...] + _dot(lo[:, 128:256], a2_ref[...]))
    g_all = _dot(_sigmoid(lo[:, 256:512]), g2_ref[...])
    if has_vres:
        v_all = v_all + (vf_ref[...] - v_all) * _sigmoid(v0_ref[...] + _dot(lo[:, 512:640], v2_ref[...]))
    else:
        vf_out_ref[...] = v_all
    kk_all = k_all * kk_ref[...]
    k_all = k_all * (1.0 + (a_all - 1.0) * ka_ref[...])
    rk_all = r_all * k_all * rk_ref[...]

    row64 = _iota2((c_len, c_len), 0)
    col64 = _iota2((c_len, c_len), 1)
    tri_incl = (row64 >= col64).astype(BF16)
    gcum_all = _dot_split(logd_all, tri_incl, 3, ones_left=True)
    p_incl_all = jnp.exp(gcum_all)
    p_prev_all = jnp.exp(gcum_all - logd_all)
    p_inv_all = jnp.exp(-gcum_all)

    r128 = _iota2((LANES, LANES), 0)
    c128 = _iota2((LANES, LANES), 1)
    same_head = (r128 >> 6) == (c128 >> 6)
    seg_ones = same_head.astype(BF16)
    strict = r128 > c128
    incl = r128 >= c128
    head0 = _iota2((c_len, LANES), 1) < n

    def seg_sum(x):
        return _dot_split(x, seg_ones, 2)

    def stack(x):
        return jnp.concatenate([jnp.where(head0, x, 0.0), jnp.where(head0, 0.0, x)], axis=0)

    grp = list(range(n_grp))
    sls = [slice(gi * LANES, (gi + 1) * LANES) for gi in grp]

    def cut(x):
        return [x[:, sl] for sl in sls]

    r, k, v, a = cut(r_all), cut(k_all), cut(v_all), cut(a_all)
    p_incl, p_prev, p_inv = cut(p_incl_all), cut(p_prev_all), cut(p_inv_all)
    kk = cut(kk_all)
    kk_ss = _each(lambda x: seg_sum(x * x), kk)
    kk = _each(lambda x, ss: x * lax.rsqrt(ss + 1e-12), kk, kk_ss)
    a_s = _each(lambda x, p: stack(-x * p), kk, p_prev)
    r_s = _each(lambda x, p: stack(x * p), r, p_incl)
    b_s = _each(lambda x, y, p: stack(x * y * p), kk, a, p_inv)
    k_s = _each(lambda x, p: stack(x * p), k, p_inv)
    v_s = _each(stack, v)

    a_ab = _each(_dot_nt, a_s, b_s)
    a_ak = _each(_dot_nt, a_s, k_s)
    a_rb = _each(_dot_nt, r_s, b_s)
    a_rk = _each(_dot_nt, r_s, k_s)
    t_inv = _unit_lower_inverse(_each(lambda m: jnp.where(strict, -m, 0.0), a_ab), 2 * c_len, c_len)
    akv = _each(lambda m, x: _dot(jnp.where(strict, m, 0.0), x), a_ak, v_s)
    u0 = _each(_dot, t_inv, akv)
    w = _each(_dot, t_inv, a_s)

    ht = [ht_ref[gi] for gi in grp]
    wh = _each(_dot_nt, w, ht)
    u = _each(lambda x, y: x + y, wh, u0)
    y_h = _each(_dot_nt, r_s, ht)
    y_u = _each(lambda m, x: _dot(jnp.where(incl, m, 0.0), x), a_rb, u)
    y_v = _each(lambda m, x: _dot(jnp.where(incl, m, 0.0), x), a_rk, v_s)
    h_u = _each(_dot_tn, u, b_s)
    h_v = _each(_dot_tn, v_s, k_s)
    y = []
    for gi in grp:
        p_last = p_incl[gi][c_len - 1:c_len, :]
        ht_ref[gi] = jnp.where(same_head, (ht[gi] + h_u[gi] + h_v[gi]) * p_last, 0.0)
        y_s = y_h[gi] + y_u[gi] + y_v[gi]
        y.append(y_s[0:c_len, :] + y_s[c_len:2 * c_len, :])

    mu = _each(lambda x: seg_sum(x) * (1.0 / n), y)
    yc = _each(lambda x, m: x - m, y, mu)
    var = _each(lambda x: seg_sum(x * x) * (1.0 / n), yc)
    bonus = _each(seg_sum, cut(rk_all))
    for gi in grp:
        sl = sls[gi]
        yn = yc[gi] * lax.rsqrt(var[gi] + RW_GN_EPS) * gw_ref[:, sl] + gb_ref[:, sl]
        o_ref[:, sl] = ((yn + bonus[gi] * v[gi]) * g_all[:, sl]).astype(o_ref.dtype)


def _rwkv7(proj, tail, r_blk0, v_first, prm, bsz, seq, n_pairs, n_grp):
    t_tot = bsz * seq
    nc = seq // CHUNK
    w_rw = n_pairs * LANES
    gw = n_grp * LANES
    assert n_pairs % n_grp == 0 and r_blk0 % n_grp == 0
    n_pg = n_pairs // n_grp
    has_vres = v_first is not None

    def col(group):
        return pl.BlockSpec((CHUNK, gw), lambda b, p, c, g=group: (b * nc + c, r_blk0 // n_grp + g * n_pg + p))

    tok = pl.BlockSpec((CHUNK, gw), lambda b, p, c: (b * nc + c, p))
    lo_spec = pl.BlockSpec((CHUNK, 5 * LANES), lambda b, p, c: (b * nc + c, 0))
    vec = pl.BlockSpec((1, gw), lambda b, p, c: (0, p))
    vec_lo = pl.BlockSpec((1, 5 * LANES), lambda b, p, c: (0, 0))

    def lowrank(rank):
        return pl.BlockSpec((rank, gw), lambda b, p, c: (0, p))

    in_specs = [col(0), col(1), col(2), lo_spec]
    args = [proj, proj, proj, tail]
    if has_vres:
        in_specs.append(tok)
        args.append(v_first)
    in_specs += [vec, vec, vec, vec_lo] + [vec] * 7 + [lowrank(128), lowrank(128), lowrank(256)]
    args += [prm["mu_r"], prm["mu_k"], prm["mu_v"], prm["mu_lo"], prm["w0"], prm["a0"], prm["k_k"],
             prm["k_a"], prm["r_k"], prm["gn_w"], prm["gn_b"], prm["w2"], prm["a2"], prm["g2"]]
    if has_vres:
        in_specs += [vec, lowrank(128)]
        args += [prm["v0"], prm["v2"]]
    out_shape = [jax.ShapeDtypeStruct((t_tot, w_rw), BF16)]
    out_specs = [tok]
    if not has_vres:
        out_shape.append(jax.ShapeDtypeStruct((t_tot, w_rw), F32))
        out_specs.append(tok)
    halo_rows = CHUNK + SUBLANES
    outs = pl.pallas_call(
        functools.partial(_rw_kernel, has_vres), grid=(bsz, n_pg, nc),
        in_specs=in_specs, out_specs=out_specs, out_shape=out_shape,
        scratch_shapes=[pltpu.VMEM((halo_rows, gw), F32)] * 3
        + [pltpu.VMEM((halo_rows, 5 * LANES), F32), pltpu.VMEM((n_grp, LANES, LANES), F32)],
        compiler_params=_cparams(("arbitrary", "arbitrary", "arbitrary")), name="rwkv7",
    )(*args)
    if has_vres:
        return outs[0], v_first
    return outs[0], outs[1]


def _layer_norm_rows(y, g, b):
    mu = jnp.mean(y, axis=1, keepdims=True)
    yc = y - mu
    var = jnp.mean(yc * yc, axis=1, keepdims=True)
    return yc * lax.rsqrt(var + LN_EPS) * g + b


def _pack_bf16_halves(x):
    half = x.shape[1] // 2
    xr = x.astype(BF16).astype(F32)
    lo = lax.shift_right_logical(lax.bitcast_convert_type(xr[:, :half], jnp.uint32), jnp.uint32(16))
    hi = lax.bitcast_convert_type(xr[:, half:], jnp.uint32) & jnp.uint32(0xFFFF0000)
    return hi | lo


def _unpack_bf16_halves(w):
    lo = lax.bitcast_convert_type(lax.shift_left(w, jnp.uint32(16)), F32)
    hi = lax.bitcast_convert_type(w & jnp.uint32(0xFFFF0000), F32)
    return jnp.concatenate([lo, hi], axis=1).astype(BF16)


def _ln_router_kernel(y_ref, g_ref, b_ref, rw_ref, rb_ref, x_ref, xp_ref, ti_ref, tw_ref, pos_ref, cnt_ref,
                      base_ref):
    tm = y_ref.shape[0]
    i = pl.program_id(0)

    @pl.when(i == 0)
    def _():
        base_ref[...] = jnp.zeros_like(base_ref)

    x = _layer_norm_rows(y_ref[...], g_ref[...], b_ref[...])
    x_ref[...] = x
    xp_ref[...] = _pack_bf16_halves(x)
    logits = _dot(x, rw_ref[...]) + rb_ref[...]
    lane = _iota2((tm, LANES), 1)
    lane_f = lane.astype(F32)
    work = logits
    sels, vals, idxs = [], [], []
    for _ in range(TOP_K):
        m = jnp.max(work, axis=1, keepdims=True)
        idx = jnp.min(jnp.where(work == m, lane_f, float(LANES)), axis=1, keepdims=True)
        sel = lane_f == idx
        sels.append(sel)
        vals.append(m)
        idxs.append(idx)
        work = jnp.where(sel, 2.0 * NEG_BIG, work)
    exps = [jnp.exp(m - vals[0]) for m in vals]
    denom = exps[0] + exps[1] + exps[2] + exps[3]
    cnt = jnp.zeros((tm, LANES), F32)
    for sel in sels:
        cnt = cnt + sel.astype(F32)
    rowt = _iota2((tm, tm), 0)
    colt = _iota2((tm, tm), 1)
    before = _dot((rowt > colt).astype(F32), cnt)
    posmat = base_ref[...] + before
    ti = jnp.zeros((tm, LANES), F32)
    tw = jnp.zeros((tm, LANES), F32)
    pos = jnp.zeros((tm, LANES), F32)
    for kk in range(TOP_K):
        here = lane == kk
        ti = jnp.where(here, idxs[kk], ti)
        tw = jnp.where(here, exps[kk] / denom, tw)
        pos = jnp.where(here, jnp.sum(jnp.where(sels[kk], posmat, 0.0), axis=1, keepdims=True), pos)
    ti_ref[...] = ti.astype(I32)
    tw_ref[...] = tw
    pos_ref[...] = pos.astype(I32)
    base_new = base_ref[...] + jnp.sum(cnt, axis=0, keepdims=True)
    base_ref[...] = base_new
    cnt_ref[...] = jnp.broadcast_to(base_new, cnt_ref.shape).astype(I32)


def _ln_router(y, g, b, rw_pad, rb_pad, tm):
    t_tot, dm = y.shape
    tm = min(tm, t_tot)
    assert t_tot % tm == 0
    row = pl.BlockSpec((tm, dm), lambda i: (i, 0))
    vec = pl.BlockSpec((1, dm), lambda i: (0, 0))
    small = pl.BlockSpec((tm, LANES), lambda i: (i, 0))
    return pl.pallas_call(
        _ln_router_kernel, grid=(t_tot // tm,),
        in_specs=[row, vec, vec, pl.BlockSpec((dm, LANES), lambda i: (0, 0)),
                  pl.BlockSpec((1, LANES), lambda i: (0, 0))],
        out_specs=[row, pl.BlockSpec((tm, dm // 2), lambda i: (i, 0)), small, small, small,
                   pl.BlockSpec((SUBLANES, LANES), lambda i: (0, 0))],
        out_shape=[jax.ShapeDtypeStruct((t_tot, dm), F32),
                   jax.ShapeDtypeStruct((t_tot, dm // 2), jnp.uint32),
                   jax.ShapeDtypeStruct((t_tot, LANES), I32),
                   jax.ShapeDtypeStruct((t_tot, LANES), F32),
                   jax.ShapeDtypeStruct((t_tot, LANES), I32),
                   jax.ShapeDtypeStruct((SUBLANES, LANES), I32)],
        scratch_shapes=[pltpu.VMEM((1, LANES), F32)],
        compiler_params=_cparams(("arbitrary",)), name="ln_router",
    )(y, g, b, rw_pad, rb_pad)


def _row_gather_start(src3, dst3, sem, idx_of_row, n_rows):
    def body(g, carry):
        for j in range(SUBLANES):
            idx = idx_of_row(g * SUBLANES + j)
            pltpu.make_async_copy(src3.at[idx >> 3, pl.ds(idx & (SUBLANES - 1), 1)],
                                  dst3.at[g, pl.ds(j, 1)], sem).start()
        return carry
    assert n_rows % SUBLANES == 0
    lax.fori_loop(0, n_rows // SUBLANES, body, 0)


def _row_gather_wait(dst3, sem):
    pltpu.make_async_copy(dst3, dst3, sem).wait()


def _moe_up_kernel(te_ref, nu_ref, tos_ref, x_hbm, wg_ref, wu_ref, bg_ref, bu_ref, h_ref, xbuf, sem):
    tm = h_ref.shape[0]
    i = pl.program_id(0)
    slot = lax.rem(i, 2)
    n_used = nu_ref[0]

    def start(tile, s):
        _row_gather_start(x_hbm, xbuf.at[s], sem.at[s], lambda r: tos_ref[tile * tm + r], tm)

    @pl.when(i == 0)
    def _():
        start(0, 0)

    @pl.when(i + 1 < n_used)
    def _():
        start(i + 1, 1 - slot)

    @pl.when(i < n_used)
    def _():
        _row_gather_wait(xbuf.at[slot], sem.at[slot])
        x = _unpack_bf16_halves(xbuf[slot].reshape(tm, xbuf.shape[-1]))
        gate = jnp.dot(x, wg_ref[...], preferred_element_type=F32) + bg_ref[...]
        up = jnp.dot(x, wu_ref[...], preferred_element_type=F32) + bu_ref[...]
        gate = jnp.minimum(gate, SWIGLU_LIMIT)
        up = jnp.clip(up, -SWIGLU_LIMIT, SWIGLU_LIMIT)
        h_ref[...] = ((up + 1.0) * gate * _sigmoid(gate * SWIGLU_ALPHA)).astype(h_ref.dtype)

    @pl.when(i >= n_used)
    def _():
        h_ref[...] = jnp.zeros_like(h_ref)


def _moe_down_kernel(te_ref, nu_ref, h_ref, wd_ref, bd_ref, y_ref, wbf):
    i = pl.program_id(0)
    new_expert = jnp.logical_or(i == 0, te_ref[i] != te_ref[jnp.maximum(i - 1, 0)])

    @pl.when(jnp.logical_and(new_expert, i < nu_ref[0]))
    def _():
        wbf[...] = wd_ref[...].astype(BF16)

    @pl.when(i < nu_ref[0])
    def _():
        y_ref[...] = (jnp.dot(h_ref[...], wbf[...], preferred_element_type=F32) + bd_ref[...]).astype(y_ref.dtype)

    @pl.when(i >= nu_ref[0])
    def _():
        y_ref[...] = jnp.zeros_like(y_ref)


def _moe_experts(xp, tile_expert, n_used, tok_of_slot, wg, wu, bg, bu, wd_all, layer, bd, tm):
    n_slots = tok_of_slot.shape[0]
    n_tiles = n_slots // tm
    _, n_exp, dm, de = wg.shape
    h = pl.pallas_call(
        _moe_up_kernel,
        grid_spec=pltpu.PrefetchScalarGridSpec(
            num_scalar_prefetch=3, grid=(n_tiles,),
            in_specs=[pl.BlockSpec(memory_space=pl.ANY),
                      pl.BlockSpec((None, None, dm, de), lambda i, te, nu, tos: (layer, te[i], 0, 0)),
                      pl.BlockSpec((None, None, dm, de), lambda i, te, nu, tos: (layer, te[i], 0, 0)),
                      pl.BlockSpec((None, 1, de), lambda i, te, nu, tos: (te[i], 0, 0)),
                      pl.BlockSpec((None, 1, de), lambda i, te, nu, tos: (te[i], 0, 0))],
            out_specs=pl.BlockSpec((tm, de), lambda i, te, nu, tos: (i, 0)),
            scratch_shapes=[pltpu.VMEM((2, tm // SUBLANES, SUBLANES, dm // 2), jnp.uint32),
                            pltpu.SemaphoreType.DMA((2,))]),
        out_shape=jax.ShapeDtypeStruct((n_slots, de), BF16),
        compiler_params=_cparams(("arbitrary",)), name="moe_gate_up",
    )(tile_expert, n_used, tok_of_slot, xp.reshape(-1, SUBLANES, dm // 2), wg, wu, bg, bu)
    return pl.pallas_call(
        _moe_down_kernel,
        grid_spec=pltpu.PrefetchScalarGridSpec(
            num_scalar_prefetch=2, grid=(n_tiles,),
            in_specs=[pl.BlockSpec((tm, de), lambda i, te, nu: (i, 0)),
                      pl.BlockSpec((None, None, de, dm), lambda i, te, nu: (layer, te[i], 0, 0)),
                      pl.BlockSpec((None, 1, dm), lambda i, te, nu: (te[i], 0, 0))],
            out_specs=pl.BlockSpec((tm, dm), lambda i, te, nu: (i, 0)),
            scratch_shapes=[pltpu.VMEM((de, dm), BF16)]),
        out_shape=jax.ShapeDtypeStruct((n_slots, dm), F32),
        compiler_params=_cparams(("arbitrary",)), name="moe_down",
    )(tile_expert, n_used, h, wd_all, bd)


def _combine_ln_kernel(alpha, dest_ref, ys_hbm, x_ref, tw_ref, g_ref, b_ref, o_ref, ob_ref, ybuf, sem):
    tm = x_ref.shape[0]
    i = pl.program_id(0)
    n = pl.num_programs(0)
    slot = lax.rem(i, 2)

    def start(tile, s):
        for kk in range(TOP_K):
            _row_gather_start(ys_hbm, ybuf.at[s, kk], sem.at[s],
                              lambda r, kk=kk: dest_ref[(tile * tm + r) * TOP_K + kk], tm)

    @pl.when(i == 0)
    def _():
        start(0, 0)

    @pl.when(i + 1 < n)
    def _():
        start(i + 1, 1 - slot)

    for kk in range(TOP_K):
        _row_gather_wait(ybuf.at[slot, kk], sem.at[slot])
    tw = tw_ref[...]
    acc = alpha * x_ref[...]
    for kk in range(TOP_K):
        acc = acc + tw[:, kk:kk + 1] * ybuf[slot, kk].reshape(tm, ybuf.shape[-1])
    out = _layer_norm_rows(acc, g_ref[...], b_ref[...])
    o_ref[...] = out
    ob_ref[...] = out.astype(BF16)


def _combine_ln(ys, x, tw, dest_flat, g, b, alpha, tm):
    t_tot, dm = x.shape
    tm = min(tm, t_tot)
    assert t_tot % tm == 0
    row = pl.BlockSpec((tm, dm), lambda i, d: (i, 0))
    vec = pl.BlockSpec((1, dm), lambda i, d: (0, 0))
    return pl.pallas_call(
        functools.partial(_combine_ln_kernel, alpha),
        grid_spec=pltpu.PrefetchScalarGridSpec(
            num_scalar_prefetch=1, grid=(t_tot // tm,),
            in_specs=[pl.BlockSpec(memory_space=pl.ANY), row,
                      pl.BlockSpec((tm, LANES), lambda i, d: (i, 0)), vec, vec],
            out_specs=[row, row],
            scratch_shapes=[pltpu.VMEM((2, TOP_K, tm // SUBLANES, SUBLANES, dm), F32),
                            pltpu.SemaphoreType.DMA((2,))]),
        out_shape=[jax.ShapeDtypeStruct((t_tot, dm), F32), jax.ShapeDtypeStruct((t_tot, dm), BF16)],
        compiler_params=_cparams(("arbitrary",)), name="moe_combine_ln",
    )(dest_flat, ys.reshape(-1, SUBLANES, dm), x, tw, g, b)


def _tiles(t_tot):
    return dict(mm_m=min(1024, t_tot), mm_n=1024, mmw_n=512, ln_m=min(256, t_tot), moe_m=min(512, t_tot),
                comb_m=min(128, t_tot), rec_grp=16)


def _pad_cols(w, width):
    return jnp.pad(w, ((0, 0), (0, width - w.shape[1])))


def _pad_rows(w, rows):
    return jnp.pad(w, ((0, rows - w.shape[0]), (0, 0)))


def kernel(x, mix_w_in, mix_w_in_vres, dn_conv_w, dn_a_log, dn_dt_bias, dn_norm_w, rw_shift_mu, rw_shift_mu_vres, rw_w0, rw_w2, rw_a0, rw_a2, rw_g2, rw_v0, rw_v2, rw_k_k, rw_k_a, rw_r_k, rw_gn_w, rw_gn_b, mix_w_out, ln1_g, ln1_b, router_w, router_b, exp_w_gate, exp_b_gate, exp_w_up, exp_b_up, exp_w_down, exp_b_down, ln2_g, ln2_b):
    bsz, seq, dm = x.shape
    depth = mix_w_in.shape[0]
    t_tot = bsz * seq
    dn_heads = dn_a_log.shape[1]
    w_dn = dn_heads * DN_HEAD_DIM
    w_rw = rw_w0.shape[1]
    n_pairs = w_rw // LANES
    r_dec, r_aaa, r_gate, r_mv = rw_w2.shape[1], rw_a2.shape[1], rw_g2.shape[1], rw_v2.shape[1]
    n_exp = router_w.shape[2]
    assert seq % CHUNK == 0 and w_rw % LANES == 0 and 2 * dn_heads <= LANES
    assert max(r_dec, r_aaa, r_mv) <= LANES and r_gate == 2 * LANES and n_exp <= LANES
    alpha = (2 * depth) ** 0.25
    tl = _tiles(t_tot)
    n_dn_in = 4 * w_dn + 2 * dn_heads

    x2d = x.reshape(t_tot, dm)
    xb = x2d.astype(BF16)
    w_in_t = jnp.swapaxes(mix_w_in, 1, 2)
    w_vres_t = jnp.swapaxes(mix_w_in_vres, 1, 2)
    wg_bf, wu_bf = exp_w_gate.astype(BF16), exp_w_up.astype(BF16)
    v_first = None
    for l in range(depth):
        rw0 = n_dn_in
        o_lo = rw0 + 3 * w_rw
        o_a, o_g = o_lo + r_dec, o_lo + r_dec + r_aaa
        v_lo_t = w_vres_t[l - 1] if l > 0 else jnp.zeros((r_mv, dm), F32)
        w_tail_t = jnp.concatenate([
            _pad_rows(w_in_t[l, o_lo:o_a], LANES), _pad_rows(w_in_t[l, o_a:o_g], LANES), w_in_t[l, o_g:o_g + r_gate],
            _pad_rows(v_lo_t, LANES), _pad_rows(w_in_t[l, 4 * w_dn:n_dn_in], LANES)], axis=0)
        proj_dn = _matmul_wcast([xb], w_in_t, l, 4 * w_dn, tl["mm_m"], tl["mm_n"], F32, w_is_t=True,
                                w_single_buffer=True, name="in_proj_dn")
        proj_rw = _matmul_wcast([xb], w_in_t[l, rw0:o_lo], 0, 3 * w_rw, tl["mm_m"], tl["mm_n"], F32, w_is_t=True,
                                w_single_buffer=True, name="in_proj_rw")
        tail = _matmul_wcast([xb], w_tail_t, 0, w_tail_t.shape[0], tl["mm_m"], w_tail_t.shape[0], F32, w_is_t=True,
                             name="in_proj_tail")

        zeros_h = jnp.zeros((dn_heads,), F32)
        alog_pad = _pad_cols(jnp.concatenate([zeros_h, dn_a_log[l]])[None, :], LANES)
        dtb_pad = _pad_cols(jnp.concatenate([zeros_h, dn_dt_bias[l]])[None, :], LANES)
        o_dn = _gated_deltanet(proj_dn, tail, 5, dn_conv_w[l], alog_pad, dtb_pad, dn_norm_w[l][None, :],
                               bsz, seq, dn_heads, min(tl["rec_grp"], dn_heads))

        mu = rw_shift_mu[l]
        mu_lo_parts = [_pad_cols(mu[None, 3 * w_rw:3 * w_rw + r_dec], LANES),
                       _pad_cols(mu[None, 3 * w_rw + r_dec:3 * w_rw + r_dec + r_aaa], LANES),
                       mu[None, 3 * w_rw + r_dec + r_aaa:],
                       _pad_cols(rw_shift_mu_vres[l - 1][None, :] if l > 0 else jnp.zeros((1, r_mv), F32), LANES)]
        prm = dict(mu_r=mu[None, :w_rw], mu_k=mu[None, w_rw:2 * w_rw], mu_v=mu[None, 2 * w_rw:3 * w_rw],
                   mu_lo=jnp.concatenate(mu_lo_parts, axis=1),
                   w0=rw_w0[l][None, :], a0=rw_a0[l][None, :], k_k=rw_k_k[l][None, :], k_a=rw_k_a[l][None, :],
                   r_k=rw_r_k[l].reshape(1, w_rw), gn_w=rw_gn_w[l][None, :], gn_b=rw_gn_b[l][None, :],
                   w2=_pad_rows(rw_w2[l], LANES), a2=_pad_rows(rw_a2[l], LANES), g2=rw_g2[l])
        if l > 0:
            prm["v0"] = rw_v0[l - 1][None, :]
            prm["v2"] = _pad_rows(rw_v2[l - 1], LANES)
        o_rw, v_first = _rwkv7(proj_rw, tail, 0, v_first if l > 0 else None, prm, bsz, seq, n_pairs,
                               min(tl["rec_grp"], n_pairs))

        y1 = _matmul_wcast([o_dn, o_rw], mix_w_out, l, dm, tl["mm_m"], tl["mmw_n"], F32, res=x2d, alpha=alpha,
                           name="out_proj")
        rw_pad = _pad_cols(router_w[l], LANES)
        rb_pad = jnp.concatenate([router_b[l], jnp.full((LANES - n_exp,), NEG_BIG, F32)])[None, :]
        x1, x1p, top_i, top_w, pos, cnt = _ln_router(y1, ln1_g[l][None, :], ln1_b[l][None, :], rw_pad, rb_pad,
                                                     tl["ln_m"])

        tm = tl["moe_m"]
        counts = cnt[0, :n_exp]
        padded = ((counts + tm - 1) // tm) * tm
        ends = jnp.cumsum(padded)
        offs = ends - padded
        n_tiles = (t_tot * TOP_K) // tm + n_exp
        dest = offs[top_i[:, :TOP_K]] + pos[:, :TOP_K]
        tok_ids = jnp.broadcast_to(jnp.arange(t_tot, dtype=I32)[:, None], (t_tot, TOP_K))
        tok_of_slot = jnp.zeros((n_tiles * tm,), I32).at[dest.reshape(-1)].set(tok_ids.reshape(-1))
        tile_start = jnp.arange(n_tiles, dtype=I32) * tm
        tile_expert = jnp.minimum(jnp.sum((ends[None, :] <= tile_start[:, None]).astype(I32), axis=1), n_exp - 1)
        n_used = (ends[-1] // tm).astype(I32)[None]

        ys = _moe_experts(x1p, tile_expert, n_used, tok_of_slot,
                          wg_bf, wu_bf,
                          exp_b_gate[l][:, None, :], exp_b_up[l][:, None, :],
                          exp_w_down, l, exp_b_down[l][:, None, :], tm)
        x2d, xb = _combine_ln(ys, x1, top_w, dest.reshape(-1), ln2_g[l][None, :], ln2_b[l][None, :],
                              alpha, tl["comb_m"])
    return x2d.reshape(bsz, seq, dm)
```

```python
import functools
import math

import jax
import jax.numpy as jnp
from jax import lax
from jax.experimental import pallas as pl
from jax.experimental.pallas import tpu as pltpu

F32 = jnp.float32
BF16 = jnp.bfloat16
I32 = jnp.int32

LANES = 128
SUBLANES = 8
VMEM_LIMIT = 56 * 1024 * 1024

CHUNK = 64
DN_HEAD_DIM = 128
DN_CONV = 4
DN_NORM_EPS = 1e-6
RW_HEAD_DIM = 64
RW_GN_EPS = 64e-5
TOP_K = 4
SWIGLU_ALPHA = 1.702
SWIGLU_LIMIT = 7.0
LN_EPS = 1e-5
NEG_BIG = -1e30


def _dot(a, b):
    return jnp.dot(a.astype(BF16), b.astype(BF16), preferred_element_type=F32)


def _dot_nt(a, b):
    return lax.dot_general(a.astype(BF16), b.astype(BF16), (((1,), (1,)), ((), ())),
                           preferred_element_type=F32)


def _dot_tn(a, b):
    return lax.dot_general(a.astype(BF16), b.astype(BF16), (((0,), (0,)), ((), ())),
                           preferred_element_type=F32)


def _dot_split(a, ones, passes, ones_left=False):
    acc, rem = None, a
    for p in range(passes):
        term = rem.astype(BF16)
        part = (jnp.dot(ones, term, preferred_element_type=F32) if ones_left
                else jnp.dot(term, ones, preferred_element_type=F32))
        acc = part if acc is None else acc + part
        if p + 1 < passes:
            rem = rem - term.astype(F32)
    return acc


def _iota2(shape, dim):
    return lax.broadcasted_iota(I32, shape, dim)


def _sigmoid(x):
    return 1.0 / (1.0 + jnp.exp(-x))


def _softplus(x):
    return jnp.maximum(x, 0.0) + jnp.log(1.0 + jnp.exp(-jnp.abs(x)))


def _silu(x):
    return x * _sigmoid(x)


def _each(fn, *lists):
    return [fn(*args) for args in zip(*lists)]


def _unit_lower_inverse(lows, n, blk):
    row = _iota2((n, n), 0)
    col = _iota2((n, n), 1)
    eye = (row == col).astype(F32)
    pair = (row >> 1) == (col >> 1)
    invs = _each(lambda low: eye - jnp.where(pair, low, 0.0), lows)
    shift = 1
    while (1 << shift) < blk:
        same_big = (row >> (shift + 1)) == (col >> (shift + 1))
        diff_small = (row >> shift) != (col >> shift)
        mask = jnp.logical_and(same_big, diff_small)
        cms = _each(lambda low: jnp.where(mask, low, 0.0), lows)
        tmp = _each(_dot, invs, cms)
        upd = _each(_dot, tmp, invs)
        invs = _each(lambda inv, d: inv - d, invs, upd)
        shift += 1
    return invs


def _cparams(sem):
    return pltpu.CompilerParams(dimension_semantics=sem, vmem_limit_bytes=VMEM_LIMIT)


def _mm_kernel(x_ref, w_ref, o_ref):
    o_ref[...] = jnp.dot(x_ref[...], w_ref[...], preferred_element_type=F32).astype(o_ref.dtype)


def _mm_res_kernel(alpha, x_ref, w_ref, r_ref, o_ref):
    acc = jnp.dot(x_ref[...], w_ref[...], preferred_element_type=F32)
    o_ref[...] = (alpha * r_ref[...] + acc).astype(o_ref.dtype)


def _matmul(x, w, tm, tn, out_dtype, res=None, alpha=1.0, name="matmul"):
    m, k = x.shape
    n = w.shape[1]
    tm = min(tm, m)
    tn = max(t for t in range(LANES, min(tn, n) + 1, LANES) if n % t == 0)
    assert m % tm == 0 and n % LANES == 0
    in_specs = [pl.BlockSpec((tm, k), lambda i, j: (i, 0)),
                pl.BlockSpec((k, tn), lambda i, j: (0, j))]
    args = [x, w]
    body = _mm_kernel
    if res is not None:
        in_specs.append(pl.BlockSpec((tm, tn), lambda i, j: (i, j)))
        args.append(res)
        body = functools.partial(_mm_res_kernel, alpha)
    return pl.pallas_call(
        body, grid=(m // tm, n // tn), in_specs=in_specs,
        out_specs=pl.BlockSpec((tm, tn), lambda i, j: (i, j)),
        out_shape=jax.ShapeDtypeStruct((m, n), out_dtype),
        compiler_params=_cparams(("arbitrary", "arbitrary")), name=name)(*args)


def _mm_wcast_kernel(n_lhs, alpha, has_res, w_is_t, *refs):
    lhs = refs[:n_lhs]
    w_ref = refs[n_lhs]
    r_ref = refs[n_lhs + 1] if has_res else None
    o_ref = refs[n_lhs + 1 + int(has_res)]
    wbf = refs[-1]

    @pl.when(pl.program_id(1) == 0)
    def _():
        wbf[...] = w_ref[...].astype(BF16)

    acc, off = None, 0
    for lr in lhs:
        kk = lr.shape[1]
        if w_is_t:
            part = lax.dot_general(lr[...], wbf[:, off:off + kk], (((1,), (1,)), ((), ())),
                                   preferred_element_type=F32)
        else:
            part = jnp.dot(lr[...], wbf[off:off + kk, :], preferred_element_type=F32)
        acc = part if acc is None else acc + part
        off += kk
    if has_res:
        acc = alpha * r_ref[...] + acc
    o_ref[...] = acc.astype(o_ref.dtype)


def _matmul_wcast(lhs_list, w, layer, n_cols, tm, tn, out_dtype, res=None, alpha=1.0, w_is_t=False,
                  w_single_buffer=False, name="matmul"):
    m = lhs_list[0].shape[0]
    k = w.shape[-1] if w_is_t else w.shape[-2]
    assert sum(a.shape[1] for a in lhs_list) == k
    tm = min(tm, m)
    tn = max(t for t in range(LANES, min(tn, n_cols) + 1, LANES) if n_cols % t == 0)
    assert m % tm == 0
    in_specs = [pl.BlockSpec((tm, a.shape[1]), lambda j, i: (i, 0)) for a in lhs_list]
    blk = (tn, k) if w_is_t else (k, tn)
    w_kw = dict(pipeline_mode=pl.Buffered(1)) if w_single_buffer else {}
    if w.ndim == 3:
        w_map = (lambda j, i: (layer, j, 0)) if w_is_t else (lambda j, i: (layer, 0, j))
        in_specs.append(pl.BlockSpec((None,) + blk, w_map, **w_kw))
    else:
        in_specs.append(pl.BlockSpec(blk, (lambda j, i: (j, 0)) if w_is_t else (lambda j, i: (0, j)), **w_kw))
    args = list(lhs_list) + [w]
    if res is not None:
        in_specs.append(pl.BlockSpec((tm, tn), lambda j, i: (i, j)))
        args.append(res)
    return pl.pallas_call(
        functools.partial(_mm_wcast_kernel, len(lhs_list), alpha, res is not None, w_is_t),
        grid=(n_cols // tn, m // tm), in_specs=in_specs,
        out_specs=pl.BlockSpec((tm, tn), lambda j, i: (i, j)),
        out_shape=jax.ShapeDtypeStruct((m, n_cols), out_dtype),
        scratch_shapes=[pltpu.VMEM(blk, BF16)],
        compiler_params=_cparams(("arbitrary", "arbitrary")), name=name)(*args)


def _dn_kernel(q_ref, k_ref, v_ref, z_ref, ba_ref, cwq_ref, cwk_ref, cwv_ref, alog_ref, dtb_ref,
               nw_ref, o_ref, qbuf, kbuf, vbuf, s_ref):
    c_len, d = CHUNK, DN_HEAD_DIM
    n_grp = s_ref.shape[0]
    gw = n_grp * d
    hg = pl.program_id(1)
    c = pl.program_id(2)
    halo = SUBLANES

    @pl.when(c == 0)
    def _():
        s_ref[...] = jnp.zeros_like(s_ref)
        for buf in (qbuf, kbuf, vbuf):
            buf[0:halo, :] = jnp.zeros((halo, gw), F32)

    @pl.when(c > 0)
    def _():
        for buf in (qbuf, kbuf, vbuf):
            buf[0:halo, :] = buf[c_len:c_len + halo, :]

    def conv_silu(buf, x_ref, cw_ref):
        x = x_ref[...]
        buf[halo:halo + c_len, :] = x
        rows = buf[...]
        acc = cw_ref[DN_CONV - 1:DN_CONV, :] * x
        for j in range(DN_CONV - 1):
            shifted = pltpu.roll(rows, DN_CONV - 1 - j, 0)[halo:halo + c_len, :]
            acc = acc + cw_ref[j:j + 1, :] * shifted
        return _silu(acc)

    q_all = conv_silu(qbuf, q_ref, cwq_ref)
    k_all = conv_silu(kbuf, k_ref, cwk_ref)
    v_all = conv_silu(vbuf, v_ref, cwv_ref)
    gate_all = nw_ref[...] * _silu(z_ref[...])

    ba = ba_ref[...]
    nh = pl.num_programs(1) * n_grp
    lane = _iota2((c_len, LANES), 1)
    row64 = _iota2((c_len, c_len), 0)
    col64 = _iota2((c_len, c_len), 1)
    tri_incl = (row64 >= col64).astype(BF16)
    causal = row64 >= col64
    strict = row64 > col64
    g_all = -jnp.exp(alog_ref[...]) * _softplus(ba + dtb_ref[...])
    gc_all = _dot_split(g_all, tri_incl, 3, ones_left=True)

    grp = list(range(n_grp))
    sls = [slice(gi * d, (gi + 1) * d) for gi in grp]

    def l2n(x):
        return x * lax.rsqrt(jnp.sum(x * x, axis=1, keepdims=True) + 1e-6)

    def pick(mat, lane_idx):
        return jnp.sum(jnp.where(lane == lane_idx, mat, 0.0), axis=1, keepdims=True)

    def decay_of(gc):
        gc_col = jnp.broadcast_to(gc, (c_len, c_len))
        gc_row = jnp.sum(jnp.where(row64 == col64, gc_col, 0.0), axis=0, keepdims=True)
        return jnp.exp(jnp.where(causal, gc_col - gc_row, NEG_BIG))

    q = [l2n(q_all[:, sl]) * (d ** -0.5) for sl in sls]
    k = [l2n(k_all[:, sl]) for sl in sls]
    v = [v_all[:, sl] for sl in sls]
    beta = [_sigmoid(pick(ba, hg * n_grp + gi)) for gi in grp]
    gc = [pick(gc_all, nh + hg * n_grp + gi) for gi in grp]
    gc_last = [x[c_len - 1:c_len, :] for x in gc]
    decay = _each(decay_of, gc)
    eg = _each(jnp.exp, gc)
    kb = _each(lambda a, b: a * b, k, beta)
    gram = _each(lambda a, b, c: _dot_nt(jnp.concatenate([a, b], axis=0), c), kb, q, k)
    low = _each(lambda m, dc: jnp.where(strict, m[0:c_len, :] * dc, 0.0), gram, decay)
    qk = _each(lambda m, dc: m[c_len:2 * c_len, :] * dc, gram, decay)
    t_inv = _unit_lower_inverse(low, c_len, c_len)
    uw = _each(lambda t, a, b, c, e: _dot(t, jnp.concatenate([a * b, c * e], axis=1)), t_inv, v, beta, kb, eg)
    u = [m[:, 0:d] for m in uw]
    w = [m[:, d:2 * d] for m in uw]
    s = [s_ref[gi] for gi in grp]
    wq_s = _each(lambda a, b, e, st: _dot(jnp.concatenate([a, b * e], axis=0), st), w, q, eg, s)
    v_new = _each(lambda a, m: a - m[0:c_len, :], u, wq_s)
    qs = [m[c_len:2 * c_len, :] for m in wq_s]
    intra = _each(_dot, qk, v_new)
    kv = _each(lambda a, gl, g1, vn: _dot_tn(a * jnp.exp(gl - g1), vn), k, gc_last, gc, v_new)
    for gi in grp:
        s_ref[gi] = s[gi] * jnp.exp(gc_last[gi]) + kv[gi]
        out = qs[gi] + intra[gi]
        out = out * lax.rsqrt(jnp.mean(out * out, axis=1, keepdims=True) + DN_NORM_EPS)
        o_ref[:, sls[gi]] = (out * gate_all[:, sls[gi]]).astype(o_ref.dtype)


def _gated_deltanet(proj, tail, ba_blk, conv_w, alog_pad, dtb_pad, norm_w, bsz, seq, n_heads, n_grp):
    t_tot = bsz * seq
    nc = seq // CHUNK
    d = DN_HEAD_DIM
    assert n_heads % n_grp == 0
    n_hg = n_heads // n_grp
    gw = n_grp * d

    def col(group):
        return pl.BlockSpec((CHUNK, gw), lambda b, h, c, g=group: (b * nc + c, g * n_hg + h))

    def cw(group):
        return pl.BlockSpec((DN_CONV, gw), lambda b, h, c, g=group: (0, g * n_hg + h))

    vec = pl.BlockSpec((1, LANES), lambda b, h, c: (0, 0))
    return pl.pallas_call(
        _dn_kernel, grid=(bsz, n_hg, nc),
        in_specs=[col(0), col(1), col(2), col(3),
                  pl.BlockSpec((CHUNK, LANES), lambda b, h, c: (b * nc + c, ba_blk)),
                  cw(0), cw(1), cw(2), vec, vec,
                  pl.BlockSpec((1, gw), lambda b, h, c: (0, 0))],
        out_specs=pl.BlockSpec((CHUNK, gw), lambda b, h, c: (b * nc + c, h)),
        out_shape=jax.ShapeDtypeStruct((t_tot, n_heads * d), BF16),
        scratch_shapes=[pltpu.VMEM((CHUNK + SUBLANES, gw), F32)] * 3 + [pltpu.VMEM((n_grp, d, d), F32)],
        compiler_params=_cparams(("arbitrary", "arbitrary", "arbitrary")), name="gated_deltanet",
    )(proj, proj, proj, proj, tail, conv_w, conv_w, conv_w, alog_pad, dtb_pad, jnp.tile(norm_w, (1, n_grp)))


def _rw_kernel(has_vres, *refs):
    if has_vres:
        (r_ref, k_ref, v_ref, lo_ref, vf_ref, mur_ref, muk_ref, muv_ref, mulo_ref, w0_ref, a0_ref,
         kk_ref, ka_ref, rk_ref, gw_ref, gb_ref, w2_ref, a2_ref, g2_ref, v0_ref, v2_ref,
         o_ref, rbuf, kbuf, vbuf, lobuf, ht_ref) = refs
    else:
        (r_ref, k_ref, v_ref, lo_ref, mur_ref, muk_ref, muv_ref, mulo_ref, w0_ref, a0_ref,
         kk_ref, ka_ref, rk_ref, gw_ref, gb_ref, w2_ref, a2_ref, g2_ref,
         o_ref, vf_out_ref, rbuf, kbuf, vbuf, lobuf, ht_ref) = refs
    c_len, n = CHUNK, RW_HEAD_DIM
    n_grp = ht_ref.shape[0]
    c = pl.program_id(2)
    halo = SUBLANES
    bufs = (rbuf, kbuf, vbuf, lobuf)

    @pl.when(c == 0)
    def _():
        ht_ref[...] = jnp.zeros_like(ht_ref)
        for buf in bufs:
            buf[0:halo, :] = jnp.zeros((halo, buf.shape[1]), F32)

    @pl.when(c > 0)
    def _():
        for buf in bufs:
            buf[0:halo, :] = buf[c_len:c_len + halo, :]

    def shift_mix(buf, x_ref, mu_ref):
        x = x_ref[...]
        buf[halo:halo + c_len, :] = x
        prev = pltpu.roll(buf[...], 1, 0)[halo:halo + c_len, :]
        return x + (prev - x) * mu_ref[...]

    r_all = shift_mix(rbuf, r_ref, mur_ref)
    k_all = shift_mix(kbuf, k_ref, muk_ref)
    v_all = shift_mix(vbuf, v_ref, muv_ref)
    lo = shift_mix(lobuf, lo_ref, mulo_ref)
    w_pre = w0_ref[...] + _dot(jnp.tanh(lo[:, 0:128]), w2_ref[...])
    w_log = -_softplus(-w_pre) - 0.5
    logd_all = -jnp.exp(w_log)
    a_all = _sigmoid(a0_ref[...] + _dot(lo[:, 128:256], a2_ref[...]))
    g_all = _dot(_sigmoid(lo[:, 256:512]), g2_ref[...])
    if has_vres:
        v_all = v_all + (vf_ref[...] - v_all) * _sigmoid(v0_ref[...] + _dot(lo[:, 512:640], v2_ref[...]))
    else:
        vf_out_ref[...] = v_all
    kk_all = k_all * kk_ref[...]
    k_all = k_all * (1.0 + (a_all - 1.0) * ka_ref[...])
    rk_all = r_all * k_all * rk_ref[...]

    row64 = _iota2((c_len, c_len), 0)
    col64 = _iota2((c_len, c_len), 1)
    tri_incl = (row64 >= col64).astype(BF16)
    gcum_all = _dot_split(logd_all, tri_incl, 3, ones_left=True)
    p_incl_all = jnp.exp(gcum_all)
    p_prev_all = jnp.exp(gcum_all - logd_all)
    p_inv_all = jnp.exp(-gcum_all)

    r128 = _iota2((LANES, LANES), 0)
    c128 = _iota2((LANES, LANES), 1)
    same_head = (r128 >> 6) == (c128 >> 6)
    seg_ones = same_head.astype(BF16)
    strict = r128 > c128
    incl = r128 >= c128
    head0 = _iota2((c_len, LANES), 1) < n

    def seg_sum(x):
        return _dot_split(x, seg_ones, 2)

    def stack(x):
        return jnp.concatenate([jnp.where(head0, x, 0.0), jnp.where(head0, 0.0, x)], axis=0)

    grp = list(range(n_grp))
    sls = [slice(gi * LANES, (gi + 1) * LANES) for gi in grp]

    def cut(x):
        return [x[:, sl] for sl in sls]

    r, k, v, a = cut(r_all), cut(k_all), cut(v_all), cut(a_all)
    p_incl, p_prev, p_inv = cut(p_incl_all), cut(p_prev_all), cut(p_inv_all)
    kk = cut(kk_all)
    kk_ss = _each(lambda x: seg_sum(x * x), kk)
    kk = _each(lambda x, ss: x * lax.rsqrt(ss + 1e-12), kk, kk_ss)
    a_s = _each(lambda x, p: stack(-x * p), kk, p_prev)
    r_s = _each(lambda x, p: stack(x * p), r, p_incl)
    b_s = _each(lambda x, y, p: stack(x * y * p), kk, a, p_inv)
    k_s = _each(lambda x, p: stack(x * p), k, p_inv)
    v_s = _each(stack, v)

    ar_s = _each(lambda x, y: jnp.concatenate([x, y], axis=0), a_s, r_s)
    m_b = _each(_dot_nt, ar_s, b_s)
    m_k = _each(_dot_nt, ar_s, k_s)
    a_ab = [m[0:2 * c_len, :] for m in m_b]
    a_rb = [m[2 * c_len:4 * c_len, :] for m in m_b]
    a_ak = [m[0:2 * c_len, :] for m in m_k]
    a_rk = [m[2 * c_len:4 * c_len, :] for m in m_k]
    t_inv = _unit_lower_inverse(_each(lambda m: jnp.where(strict, -m, 0.0), a_ab), 2 * c_len, c_len)
    akv = _each(lambda m, x: _dot(jnp.where(strict, m, 0.0), x), a_ak, v_s)
    tw = _each(lambda t, x, y: _dot(t, jnp.concatenate([x, y], axis=1)), t_inv, akv, a_s)
    u0 = [m[:, 0:LANES] for m in tw]
    w = [m[:, LANES:2 * LANES] for m in tw]

    ht = [ht_ref[gi] for gi in grp]
    wr_h = _each(lambda x, y, h: _dot_nt(jnp.concatenate([x, y], axis=0), h), w, r_s, ht)
    u = _each(lambda m, x: m[0:2 * c_len, :] + x, wr_h, u0)
    y_h = [m[2 * c_len:4 * c_len, :] for m in wr_h]
    uv = _each(lambda x, y: jnp.concatenate([x, y], axis=0), u, v_s)
    y_uv = _each(lambda mb, mk, x: _dot(jnp.concatenate([jnp.where(incl, mb, 0.0), jnp.where(incl, mk, 0.0)],
                                                        axis=1), x), a_rb, a_rk, uv)
    h_uv = _each(lambda x, b, k: _dot_tn(x, jnp.concatenate([b, k], axis=0)), uv, b_s, k_s)
    y = []
    for gi in grp:
        p_last = p_incl[gi][c_len - 1:c_len, :]
        ht_ref[gi] = jnp.where(same_head, (ht[gi] + h_uv[gi]) * p_last, 0.0)
        y_s = y_h[gi] + y_uv[gi]
        y.append(y_s[0:c_len, :] + y_s[c_len:2 * c_len, :])

    mu = _each(lambda x: seg_sum(x) * (1.0 / n), y)
    yc = _each(lambda x, m: x - m, y, mu)
    var = _each(lambda x: seg_sum(x * x) * (1.0 / n), yc)
    bonus = _each(seg_sum, cut(rk_all))
    for gi in grp:
        sl = sls[gi]
        yn = yc[gi] * lax.rsqrt(var[gi] + RW_GN_EPS) * gw_ref[:, sl] + gb_ref[:, sl]
        o_ref[:, sl] = ((yn + bonus[gi] * v[gi]) * g_all[:, sl]).astype(o_ref.dtype)


def _rwkv7(proj, tail, r_blk0, v_first, prm, bsz, seq, n_pairs, n_grp):
    t_tot = bsz * seq
    nc = seq // CHUNK
    w_rw = n_pairs * LANES
    gw = n_grp * LANES
    assert n_pairs % n_grp == 0 and r_blk0 % n_grp == 0
    n_pg = n_pairs // n_grp
    has_vres = v_first is not None

    def col(group):
        return pl.BlockSpec((CHUNK, gw), lambda b, p, c, g=group: (b * nc + c, r_blk0 // n_grp + g * n_pg + p))

    tok = pl.BlockSpec((CHUNK, gw), lambda b, p, c: (b * nc + c, p))
    lo_spec = pl.BlockSpec((CHUNK, 5 * LANES), lambda b, p, c: (b * nc + c, 0))
    vec = pl.BlockSpec((1, gw), lambda b, p, c: (0, p))
    vec_lo = pl.BlockSpec((1, 5 * LANES), lambda b, p, c: (0, 0))

    def lowrank(rank):
        return pl.BlockSpec((rank, gw), lambda b, p, c: (0, p))

    in_specs = [col(0), col(1), col(2), lo_spec]
    args = [proj, proj, proj, tail]
    if has_vres:
        in_specs.append(tok)
        args.append(v_first)
    in_specs += [vec, vec, vec, vec_lo] + [vec] * 7 + [lowrank(128), lowrank(128), lowrank(256)]
    args += [prm["mu_r"], prm["mu_k"], prm["mu_v"], prm["mu_lo"], prm["w0"], prm["a0"], prm["k_k"],
             prm["k_a"], prm["r_k"], prm["gn_w"], prm["gn_b"], prm["w2"], prm["a2"], prm["g2"]]
    if has_vres:
        in_specs += [vec, lowrank(128)]
        args += [prm["v0"], prm["v2"]]
    out_shape = [jax.ShapeDtypeStruct((t_tot, w_rw), BF16)]
    out_specs = [tok]
    if not has_vres:
        out_shape.append(jax.ShapeDtypeStruct((t_tot, w_rw), F32))
        out_specs.append(tok)
    halo_rows = CHUNK + SUBLANES
    outs = pl.pallas_call(
        functools.partial(_rw_kernel, has_vres), grid=(bsz, n_pg, nc),
        in_specs=in_specs, out_specs=out_specs, out_shape=out_shape,
        scratch_shapes=[pltpu.VMEM((halo_rows, gw), F32)] * 3
        + [pltpu.VMEM((halo_rows, 5 * LANES), F32), pltpu.VMEM((n_grp, LANES, LANES), F32)],
        compiler_params=_cparams(("arbitrary", "arbitrary", "arbitrary")), name="rwkv7",
    )(*args)
    if has_vres:
        return outs[0], v_first
    return outs[0], outs[1]


def _layer_norm_rows(y, g, b):
    mu = jnp.mean(y, axis=1, keepdims=True)
    yc = y - mu
    var = jnp.mean(yc * yc, axis=1, keepdims=True)
    return yc * lax.rsqrt(var + LN_EPS) * g + b


def _pack_bf16_halves(x):
    half = x.shape[1] // 2
    xr = x.astype(BF16).astype(F32)
    lo = lax.shift_right_logical(lax.bitcast_convert_type(xr[:, :half], jnp.uint32), jnp.uint32(16))
    hi = lax.bitcast_convert_type(xr[:, half:], jnp.uint32) & jnp.uint32(0xFFFF0000)
    return hi | lo


def _unpack_bf16_halves(w):
    lo = lax.bitcast_convert_type(lax.shift_left(w, jnp.uint32(16)), F32)
    hi = lax.bitcast_convert_type(w & jnp.uint32(0xFFFF0000), F32)
    return jnp.concatenate([lo, hi], axis=1).astype(BF16)


def _ln_router_kernel(y_ref, g_ref, b_ref, rw_ref, rb_ref, x_ref, xp_ref, ti_ref, tw_ref, pos_ref, cnt_ref,
                      base_ref):
    tm = y_ref.shape[0]
    i = pl.program_id(0)

    @pl.when(i == 0)
    def _():
        base_ref[...] = jnp.zeros_like(base_ref)

    x = _layer_norm_rows(y_ref[...], g_ref[...], b_ref[...])
    x_ref[...] = x
    xp_ref[...] = _pack_bf16_halves(x)
    logits = _dot(x, rw_ref[...]) + rb_ref[...]
    lane = _iota2((tm, LANES), 1)
    lane_f = lane.astype(F32)
    work = logits
    sels, vals, idxs = [], [], []
    for _ in range(TOP_K):
        m = jnp.max(work, axis=1, keepdims=True)
        idx = jnp.min(jnp.where(work == m, lane_f, float(LANES)), axis=1, keepdims=True)
        sel = lane_f == idx
        sels.append(sel)
        vals.append(m)
        idxs.append(idx)
        work = jnp.where(sel, 2.0 * NEG_BIG, work)
    exps = [jnp.exp(m - vals[0]) for m in vals]
    denom = exps[0] + exps[1] + exps[2] + exps[3]
    cnt = jnp.zeros((tm, LANES), F32)
    for sel in sels:
        cnt = cnt + sel.astype(F32)
    rowt = _iota2((tm, tm), 0)
    colt = _iota2((tm, tm), 1)
    before = _dot((rowt > colt).astype(F32), cnt)
    posmat = base_ref[...] + before
    ti = jnp.zeros((tm, LANES), F32)
    tw = jnp.zeros((tm, LANES), F32)
    pos = jnp.zeros((tm, LANES), F32)
    for kk in range(TOP_K):
        here = lane == kk
        ti = jnp.where(here, idxs[kk], ti)
        tw = jnp.where(here, exps[kk] / denom, tw)
        pos = jnp.where(here, jnp.sum(jnp.where(sels[kk], posmat, 0.0), axis=1, keepdims=True), pos)
    ti_ref[...] = ti.astype(I32)
    tw_ref[...] = tw
    pos_ref[...] = pos.astype(I32)
    base_new = base_ref[...] + jnp.sum(cnt, axis=0, keepdims=True)
    base_ref[...] = base_new
    cnt_ref[...] = jnp.broadcast_to(base_new, cnt_ref.shape).astype(I32)


def _ln_router(y, g, b, rw_pad, rb_pad, tm):
    t_tot, dm = y.shape
    tm = min(tm, t_tot)
    assert t_tot % tm == 0
    row = pl.BlockSpec((tm, dm), lambda i: (i, 0))
    vec = pl.BlockSpec((1, dm), lambda i: (0, 0))
    small = pl.BlockSpec((tm, LANES), lambda i: (i, 0))
    return pl.pallas_call(
        _ln_router_kernel, grid=(t_tot // tm,),
        in_specs=[row, vec, vec, pl.BlockSpec((dm, LANES), lambda i: (0, 0)),
                  pl.BlockSpec((1, LANES), lambda i: (0, 0))],
        out_specs=[row, pl.BlockSpec((tm, dm // 2), lambda i: (i, 0)), small, small, small,
                   pl.BlockSpec((SUBLANES, LANES), lambda i: (0, 0))],
        out_shape=[jax.ShapeDtypeStruct((t_tot, dm), F32),
                   jax.ShapeDtypeStruct((t_tot, dm // 2), jnp.uint32),
                   jax.ShapeDtypeStruct((t_tot, LANES), I32),
                   jax.ShapeDtypeStruct((t_tot, LANES), F32),
                   jax.ShapeDtypeStruct((t_tot, LANES), I32),
                   jax.ShapeDtypeStruct((SUBLANES, LANES), I32)],
        scratch_shapes=[pltpu.VMEM((1, LANES), F32)],
        compiler_params=_cparams(("arbitrary",)), name="ln_router",
    )(y, g, b, rw_pad, rb_pad)


def _row_gather_start(src3, dst3, sem, idx_of_row, n_rows):
    def body(g, carry):
        for j in range(SUBLANES):
            idx = idx_of_row(g * SUBLANES + j)
            pltpu.make_async_copy(src3.at[idx >> 3, pl.ds(idx & (SUBLANES - 1), 1)],
                                  dst3.at[g, pl.ds(j, 1)], sem).start()
        return carry
    assert n_rows % SUBLANES == 0
    lax.fori_loop(0, n_rows // SUBLANES, body, 0)


def _row_gather_wait(dst3, sem):
    pltpu.make_async_copy(dst3, dst3, sem).wait()


def _moe_up_kernel(te_ref, nu_ref, tos_ref, x_hbm, wg_ref, wu_ref, bg_ref, bu_ref, h_ref, xbuf, sem):
    tm = h_ref.shape[0]
    i = pl.program_id(0)
    slot = lax.rem(i, 2)
    n_used = nu_ref[0]

    def start(tile, s):
        _row_gather_start(x_hbm, xbuf.at[s], sem.at[s], lambda r: tos_ref[tile * tm + r], tm)

    @pl.when(i == 0)
    def _():
        start(0, 0)

    @pl.when(i + 1 < n_used)
    def _():
        start(i + 1, 1 - slot)

    @pl.when(i < n_used)
    def _():
        _row_gather_wait(xbuf.at[slot], sem.at[slot])
        x = _unpack_bf16_halves(xbuf[slot].reshape(tm, xbuf.shape[-1]))
        gate = jnp.dot(x, wg_ref[...], preferred_element_type=F32) + bg_ref[...]
        up = jnp.dot(x, wu_ref[...], preferred_element_type=F32) + bu_ref[...]
        gate = jnp.minimum(gate, SWIGLU_LIMIT)
        up = jnp.clip(up, -SWIGLU_LIMIT, SWIGLU_LIMIT)
        h_ref[...] = ((up + 1.0) * gate * _sigmoid(gate * SWIGLU_ALPHA)).astype(h_ref.dtype)

    @pl.when(i >= n_used)
    def _():
        h_ref[...] = jnp.zeros_like(h_ref)


def _moe_down_kernel(te_ref, nu_ref, h_ref, wd_ref, bd_ref, y_ref, wbf):
    i = pl.program_id(0)
    new_expert = jnp.logical_or(i == 0, te_ref[i] != te_ref[jnp.maximum(i - 1, 0)])

    @pl.when(jnp.logical_and(new_expert, i < nu_ref[0]))
    def _():
        wbf[...] = wd_ref[...].astype(BF16)

    @pl.when(i < nu_ref[0])
    def _():
        y_ref[...] = (jnp.dot(h_ref[...], wbf[...], preferred_element_type=F32) + bd_ref[...]).astype(y_ref.dtype)

    @pl.when(i >= nu_ref[0])
    def _():
        y_ref[...] = jnp.zeros_like(y_ref)


def _moe_experts(xp, tile_expert, n_used, tok_of_slot, wg, wu, bg, bu, wd_all, layer, bd, tm):
    n_slots = tok_of_slot.shape[0]
    n_tiles = n_slots // tm
    _, n_exp, dm, de = wg.shape
    h = pl.pallas_call(
        _moe_up_kernel,
        grid_spec=pltpu.PrefetchScalarGridSpec(
            num_scalar_prefetch=3, grid=(n_tiles,),
            in_specs=[pl.BlockSpec(memory_space=pl.ANY),
                      pl.BlockSpec((None, None, dm, de), lambda i, te, nu, tos: (layer, te[i], 0, 0)),
                      pl.BlockSpec((None, None, dm, de), lambda i, te, nu, tos: (layer, te[i], 0, 0)),
                      pl.BlockSpec((None, 1, de), lambda i, te, nu, tos: (te[i], 0, 0)),
                      pl.BlockSpec((None, 1, de), lambda i, te, nu, tos: (te[i], 0, 0))],
            out_specs=pl.BlockSpec((tm, de), lambda i, te, nu, tos: (i, 0)),
            scratch_shapes=[pltpu.VMEM((2, tm // SUBLANES, SUBLANES, dm // 2), jnp.uint32),
                            pltpu.SemaphoreType.DMA((2,))]),
        out_shape=jax.ShapeDtypeStruct((n_slots, de), BF16),
        compiler_params=_cparams(("arbitrary",)), name="moe_gate_up",
    )(tile_expert, n_used, tok_of_slot, xp.reshape(-1, SUBLANES, dm // 2), wg, wu, bg, bu)
    return pl.pallas_call(
        _moe_down_kernel,
        grid_spec=pltpu.PrefetchScalarGridSpec(
            num_scalar_prefetch=2, grid=(n_tiles,),
            in_specs=[pl.BlockSpec((tm, de), lambda i, te, nu: (i, 0)),
                      pl.BlockSpec((None, None, de, dm), lambda i, te, nu: (layer, te[i], 0, 0)),
                      pl.BlockSpec((None, 1, dm), lambda i, te, nu: (te[i], 0, 0))],
            out_specs=pl.BlockSpec((tm, dm), lambda i, te, nu: (i, 0)),
            scratch_shapes=[pltpu.VMEM((de, dm), BF16)]),
        out_shape=jax.ShapeDtypeStruct((n_slots, dm), F32),
        compiler_params=_cparams(("arbitrary",)), name="moe_down",
    )(tile_expert, n_used, h, wd_all, bd)


def _combine_ln_kernel(alpha, dest_ref, ys_hbm, x_ref, tw_ref, g_ref, b_ref, o_ref, ob_ref, ybuf, sem):
    tm = x_ref.shape[0]
    i = pl.program_id(0)
    n = pl.num_programs(0)
    slot = lax.rem(i, 2)

    def start(tile, s):
        for kk in range(TOP_K):
            _row_gather_start(ys_hbm, ybuf.at[s, kk], sem.at[s],
                              lambda r, kk=kk: dest_ref[(tile * tm + r) * TOP_K + kk], tm)

    @pl.when(i == 0)
    def _():
        start(0, 0)

    @pl.when(i + 1 < n)
    def _():
        start(i + 1, 1 - slot)

    for kk in range(TOP_K):
        _row_gather_wait(ybuf.at[slot, kk], sem.at[slot])
    tw = tw_ref[...]
    acc = alpha * x_ref[...]
    for kk in range(TOP_K):
        acc = acc + tw[:, kk:kk + 1] * ybuf[slot, kk].reshape(tm, ybuf.shape[-1])
    out = _layer_norm_rows(acc, g_ref[...], b_ref[...])
    o_ref[...] = out
    ob_ref[...] = out.astype(BF16)


def _combine_ln(ys, x, tw, dest_flat, g, b, alpha, tm):
    t_tot, dm = x.shape
    tm = min(tm, t_tot)
    assert t_tot % tm == 0
    row = pl.BlockSpec((tm, dm), lambda i, d: (i, 0))
    vec = pl.BlockSpec((1, dm), lambda i, d: (0, 0))
    return pl.pallas_call(
        functools.partial(_combine_ln_kernel, alpha),
        grid_spec=pltpu.PrefetchScalarGridSpec(
            num_scalar_prefetch=1, grid=(t_tot // tm,),
            in_specs=[pl.BlockSpec(memory_space=pl.ANY), row,
                      pl.BlockSpec((tm, LANES), lambda i, d: (i, 0)), vec, vec],
            out_specs=[row, row],
            scratch_shapes=[pltpu.VMEM((2, TOP_K, tm // SUBLANES, SUBLANES, dm), F32),
                            pltpu.SemaphoreType.DMA((2,))]),
        out_shape=[jax.ShapeDtypeStruct((t_tot, dm), F32), jax.ShapeDtypeStruct((t_tot, dm), BF16)],
        compiler_params=_cparams(("arbitrary",)), name="moe_combine_ln",
    )(dest_flat, ys.reshape(-1, SUBLANES, dm), x, tw, g, b)


def _tiles(t_tot):
    return dict(mm_m=min(1024, t_tot), mm_n=1024, mmw_n=512, ln_m=min(256, t_tot), moe_m=min(512, t_tot),
                comb_m=min(128, t_tot), rec_grp=16)


def _pad_cols(w, width):
    return jnp.pad(w, ((0, 0), (0, width - w.shape[1])))


def _pad_rows(w, rows):
    return jnp.pad(w, ((0, rows - w.shape[0]), (0, 0)))


def kernel(x, mix_w_in, mix_w_in_vres, dn_conv_w, dn_a_log, dn_dt_bias, dn_norm_w, rw_shift_mu, rw_shift_mu_vres, rw_w0, rw_w2, rw_a0, rw_a2, rw_g2, rw_v0, rw_v2, rw_k_k, rw_k_a, rw_r_k, rw_gn_w, rw_gn_b, mix_w_out, ln1_g, ln1_b, router_w, router_b, exp_w_gate, exp_b_gate, exp_w_up, exp_b_up, exp_w_down, exp_b_down, ln2_g, ln2_b):
    bsz, seq, dm = x.shape
    depth = mix_w_in.shape[0]
    t_tot = bsz * seq
    dn_heads = dn_a_log.shape[1]
    w_dn = dn_heads * DN_HEAD_DIM
    w_rw = rw_w0.shape[1]
    n_pairs = w_rw // LANES
    r_dec, r_aaa, r_gate, r_mv = rw_w2.shape[1], rw_a2.shape[1], rw_g2.shape[1], rw_v2.shape[1]
    n_exp = router_w.shape[2]
    assert seq % CHUNK == 0 and w_rw % LANES == 0 and 2 * dn_heads <= LANES
    assert max(r_dec, r_aaa, r_mv) <= LANES and r_gate == 2 * LANES and n_exp <= LANES
    alpha = (2 * depth) ** 0.25
    tl = _tiles(t_tot)
    n_dn_in = 4 * w_dn + 2 * dn_heads

    x2d = x.reshape(t_tot, dm)
    xb = x2d.astype(BF16)
    w_in_t = jnp.swapaxes(mix_w_in, 1, 2)
    w_vres_t = jnp.swapaxes(mix_w_in_vres, 1, 2)
    wg_bf, wu_bf = exp_w_gate.astype(BF16), exp_w_up.astype(BF16)
    v_first = None
    for l in range(depth):
        rw0 = n_dn_in
        o_lo = rw0 + 3 * w_rw
        o_a, o_g = o_lo + r_dec, o_lo + r_dec + r_aaa
        v_lo_t = w_vres_t[l - 1] if l > 0 else jnp.zeros((r_mv, dm), F32)
        w_tail_t = jnp.concatenate([
            _pad_rows(w_in_t[l, o_lo:o_a], LANES), _pad_rows(w_in_t[l, o_a:o_g], LANES), w_in_t[l, o_g:o_g + r_gate],
            _pad_rows(v_lo_t, LANES), _pad_rows(w_in_t[l, 4 * w_dn:n_dn_in], LANES)], axis=0)
        proj_dn = _matmul_wcast([xb], w_in_t, l, 4 * w_dn, tl["mm_m"], tl["mm_n"], F32, w_is_t=True,
                                w_single_buffer=True, name="in_proj_dn")
        proj_rw = _matmul_wcast([xb], w_in_t[l, rw0:o_lo], 0, 3 * w_rw, tl["mm_m"], tl["mm_n"], F32, w_is_t=True,
                                w_single_buffer=True, name="in_proj_rw")
        tail = _matmul_wcast([xb], w_tail_t, 0, w_tail_t.shape[0], tl["mm_m"], w_tail_t.shape[0], F32, w_is_t=True,
                             name="in_proj_tail")

        zeros_h = jnp.zeros((dn_heads,), F32)
        alog_pad = _pad_cols(jnp.concatenate([zeros_h, dn_a_log[l]])[None, :], LANES)
        dtb_pad = _pad_cols(jnp.concatenate([zeros_h, dn_dt_bias[l]])[None, :], LANES)
        o_dn = _gated_deltanet(proj_dn, tail, 5, dn_conv_w[l], alog_pad, dtb_pad, dn_norm_w[l][None, :],
                               bsz, seq, dn_heads, min(tl["rec_grp"], dn_heads))

        mu = rw_shift_mu[l]
        mu_lo_parts = [_pad_cols(mu[None, 3 * w_rw:3 * w_rw + r_dec], LANES),
                       _pad_cols(mu[None, 3 * w_rw + r_dec:3 * w_rw + r_dec + r_aaa], LANES),
                       mu[None, 3 * w_rw + r_dec + r_aaa:],
                       _pad_cols(rw_shift_mu_vres[l - 1][None, :] if l > 0 else jnp.zeros((1, r_mv), F32), LANES)]
        prm = dict(mu_r=mu[None, :w_rw], mu_k=mu[None, w_rw:2 * w_rw], mu_v=mu[None, 2 * w_rw:3 * w_rw],
                   mu_lo=jnp.concatenate(mu_lo_parts, axis=1),
                   w0=rw_w0[l][None, :], a0=rw_a0[l][None, :], k_k=rw_k_k[l][None, :], k_a=rw_k_a[l][None, :],
                   r_k=rw_r_k[l].reshape(1, w_rw), gn_w=rw_gn_w[l][None, :], gn_b=rw_gn_b[l][None, :],
                   w2=_pad_rows(rw_w2[l], LANES), a2=_pad_rows(rw_a2[l], LANES), g2=rw_g2[l])
        if l > 0:
            prm["v0"] = rw_v0[l - 1][None, :]
            prm["v2"] = _pad_rows(rw_v2[l - 1], LANES)
        o_rw, v_first = _rwkv7(proj_rw, tail, 0, v_first if l > 0 else None, prm, bsz, seq, n_pairs,
                               min(tl["rec_grp"], n_pairs))

        y1 = _matmul_wcast([o_dn, o_rw], mix_w_out, l, dm, tl["mm_m"], tl["mmw_n"], F32, res=x2d, alpha=alpha,
                           name="out_proj")
        rw_pad = _pad_cols(router_w[l], LANES)
        rb_pad = jnp.concatenate([router_b[l], jnp.full((LANES - n_exp,), NEG_BIG, F32)])[None, :]
        x1, x1p, top_i, top_w, pos, cnt = _ln_router(y1, ln1_g[l][None, :], ln1_b[l][None, :], rw_pad, rb_pad,
                                                     tl["ln_m"])

        tm = tl["moe_m"]
        counts = cnt[0, :n_exp]
        padded = ((counts + tm - 1) // tm) * tm
        ends = jnp.cumsum(padded)
        offs = ends - padded
        n_tiles = (t_tot * TOP_K) // tm + n_exp
        dest = offs[top_i[:, :TOP_K]] + pos[:, :TOP_K]
        tok_ids = jnp.broadcast_to(jnp.arange(t_tot, dtype=I32)[:, None], (t_tot, TOP_K))
        tok_of_slot = jnp.zeros((n_tiles * tm,), I32).at[dest.reshape(-1)].set(tok_ids.reshape(-1))
        tile_start = jnp.arange(n_tiles, dtype=I32) * tm
        tile_expert = jnp.minimum(jnp.sum((ends[None, :] <= tile_start[:, None]).astype(I32), axis=1), n_exp - 1)
        n_used = (ends[-1] // tm).astype(I32)[None]

        ys = _moe_experts(x1p, tile_expert, n_used, tok_of_slot,
                          wg_bf, wu_bf,
                          exp_b_gate[l][:, None, :], exp_b_up[l][:, None, :],
                          exp_w_down, l, exp_b_down[l][:, None, :], tm)
        x2d, xb = _combine_ln(ys, x1, top_w, dest.reshape(-1), ln2_g[l][None, :], ln2_b[l][None, :],
                              alpha, tl["comb_m"])
    return x2d.reshape(bsz, seq, dm)
```

```python
import functools
import math

import jax
import jax.numpy as jnp
from jax import lax
from jax.experimental import pallas as pl
from jax.experimental.pallas import tpu as pltpu

F32 = jnp.float32
BF16 = jnp.bfloat16
I32 = jnp.int32

LANES = 128
SUBLANES = 8
VMEM_LIMIT = 56 * 1024 * 1024

CHUNK = 64
DN_HEAD_DIM = 128
DN_CONV = 4
DN_NORM_EPS = 1e-6
RW_HEAD_DIM = 64
RW_GN_EPS = 64e-5
TOP_K = 4
SWIGLU_ALPHA = 1.702
SWIGLU_LIMIT = 7.0
LN_EPS = 1e-5
NEG_BIG = -1e30


def _dot(a, b):
    return jnp.dot(a.astype(BF16), b.astype(BF16), preferred_element_type=F32)


def _dot_nt(a, b):
    return lax.dot_general(a.astype(BF16), b.astype(BF16), (((1,), (1,)), ((), ())),
                           preferred_element_type=F32)


def _dot_tn(a, b):
    return lax.dot_general(a.astype(BF16), b.astype(BF16), (((0,), (0,)), ((), ())),
                           preferred_element_type=F32)


def _dot_split(a, ones, passes, ones_left=False):
    terms, rem = [], a
    for p in range(passes):
        term = rem.astype(BF16)
        terms.append(term)
        if p + 1 < passes:
            rem = rem - term.astype(F32)
    if ones_left:
        acc = None
        for term in terms:
            part = jnp.dot(ones, term, preferred_element_type=F32)
            acc = part if acc is None else acc + part
        return acc
    m = a.shape[0]
    prod = jnp.dot(jnp.concatenate(terms, axis=0), ones, preferred_element_type=F32)
    acc = prod[0:m, :]
    for p in range(1, passes):
        acc = acc + prod[p * m:(p + 1) * m, :]
    return acc


def _iota2(shape, dim):
    return lax.broadcasted_iota(I32, shape, dim)


def _sigmoid(x):
    return 1.0 / (1.0 + jnp.exp(-x))


def _softplus(x):
    return jnp.maximum(x, 0.0) + jnp.log(1.0 + jnp.exp(-jnp.abs(x)))


def _silu(x):
    return x * _sigmoid(x)


def _each(fn, *lists):
    return [fn(*args) for args in zip(*lists)]


def _unit_lower_inverse(lows, n, blk):
    row = _iota2((n, n), 0)
    col = _iota2((n, n), 1)
    eye = (row == col).astype(F32)
    pair = (row >> 1) == (col >> 1)
    invs = _each(lambda low: eye - jnp.where(pair, low, 0.0), lows)
    shift = 1
    while (1 << shift) < blk:
        same_big = (row >> (shift + 1)) == (col >> (shift + 1))
        diff_small = (row >> shift) != (col >> shift)
        mask = jnp.logical_and(same_big, diff_small)
        cms = _each(lambda low: jnp.where(mask, low, 0.0), lows)
        tmp = _each(_dot, invs, cms)
        upd = _each(_dot, tmp, invs)
        invs = _each(lambda inv, d: inv - d, invs, upd)
        shift += 1
    return invs


def _cparams(sem):
    return pltpu.CompilerParams(dimension_semantics=sem, vmem_limit_bytes=VMEM_LIMIT)


def _mm_kernel(x_ref, w_ref, o_ref):
    o_ref[...] = jnp.dot(x_ref[...], w_ref[...], preferred_element_type=F32).astype(o_ref.dtype)


def _mm_res_kernel(alpha, x_ref, w_ref, r_ref, o_ref):
    acc = jnp.dot(x_ref[...], w_ref[...], preferred_element_type=F32)
    o_ref[...] = (alpha * r_ref[...] + acc).astype(o_ref.dtype)


def _matmul(x, w, tm, tn, out_dtype, res=None, alpha=1.0, name="matmul"):
    m, k = x.shape
    n = w.shape[1]
    tm = min(tm, m)
    tn = max(t for t in range(LANES, min(tn, n) + 1, LANES) if n % t == 0)
    assert m % tm == 0 and n % LANES == 0
    in_specs = [pl.BlockSpec((tm, k), lambda i, j: (i, 0)),
                pl.BlockSpec((k, tn), lambda i, j: (0, j))]
    args = [x, w]
    body = _mm_kernel
    if res is not None:
        in_specs.append(pl.BlockSpec((tm, tn), lambda i, j: (i, j)))
        args.append(res)
        body = functools.partial(_mm_res_kernel, alpha)
    return pl.pallas_call(
        body, grid=(m // tm, n // tn), in_specs=in_specs,
        out_specs=pl.BlockSpec((tm, tn), lambda i, j: (i, j)),
        out_shape=jax.ShapeDtypeStruct((m, n), out_dtype),
        compiler_params=_cparams(("arbitrary", "arbitrary")), name=name)(*args)


def _mm_wcast_kernel(n_lhs, alpha, has_res, w_is_t, *refs):
    lhs = refs[:n_lhs]
    w_ref = refs[n_lhs]
    r_ref = refs[n_lhs + 1] if has_res else None
    o_ref = refs[n_lhs + 1 + int(has_res)]
    wbf = refs[-1]

    @pl.when(pl.program_id(1) == 0)
    def _():
        wbf[...] = w_ref[...].astype(BF16)

    acc, off = None, 0
    for lr in lhs:
        kk = lr.shape[1]
        if w_is_t:
            part = lax.dot_general(lr[...], wbf[:, off:off + kk], (((1,), (1,)), ((), ())),
                                   preferred_element_type=F32)
        else:
            part = jnp.dot(lr[...], wbf[off:off + kk, :], preferred_element_type=F32)
        acc = part if acc is None else acc + part
        off += kk
    if has_res:
        acc = alpha * r_ref[...] + acc
    o_ref[...] = acc.astype(o_ref.dtype)


def _matmul_wcast(lhs_list, w, layer, n_cols, tm, tn, out_dtype, res=None, alpha=1.0, w_is_t=False,
                  w_single_buffer=False, name="matmul"):
    m = lhs_list[0].shape[0]
    k = w.shape[-1] if w_is_t else w.shape[-2]
    assert sum(a.shape[1] for a in lhs_list) == k
    tm = min(tm, m)
    tn = max(t for t in range(LANES, min(tn, n_cols) + 1, LANES) if n_cols % t == 0)
    assert m % tm == 0
    in_specs = [pl.BlockSpec((tm, a.shape[1]), lambda j, i: (i, 0)) for a in lhs_list]
    blk = (tn, k) if w_is_t else (k, tn)
    w_kw = dict(pipeline_mode=pl.Buffered(1)) if w_single_buffer else {}
    if w.ndim == 3:
        w_map = (lambda j, i: (layer, j, 0)) if w_is_t else (lambda j, i: (layer, 0, j))
        in_specs.append(pl.BlockSpec((None,) + blk, w_map, **w_kw))
    else:
        in_specs.append(pl.BlockSpec(blk, (lambda j, i: (j, 0)) if w_is_t else (lambda j, i: (0, j)), **w_kw))
    args = list(lhs_list) + [w]
    if res is not None:
        in_specs.append(pl.BlockSpec((tm, tn), lambda j, i: (i, j)))
        args.append(res)
    return pl.pallas_call(
        functools.partial(_mm_wcast_kernel, len(lhs_list), alpha, res is not None, w_is_t),
        grid=(n_cols // tn, m // tm), in_specs=in_specs,
        out_specs=pl.BlockSpec((tm, tn), lambda j, i: (i, j)),
        out_shape=jax.ShapeDtypeStruct((m, n_cols), out_dtype),
        scratch_shapes=[pltpu.VMEM(blk, BF16)],
        compiler_params=_cparams(("arbitrary", "arbitrary")), name=name)(*args)


def _dn_kernel(q_ref, k_ref, v_ref, z_ref, ba_ref, cwq_ref, cwk_ref, cwv_ref, alog_ref, dtb_ref,
               nw_ref, o_ref, qbuf, kbuf, vbuf, s_ref):
    c_len, d = CHUNK, DN_HEAD_DIM
    n_grp = s_ref.shape[0]
    gw = n_grp * d
    hg = pl.program_id(1)
    c = pl.program_id(2)
    halo = SUBLANES

    @pl.when(c == 0)
    def _():
        s_ref[...] = jnp.zeros_like(s_ref)
        for buf in (qbuf, kbuf, vbuf):
            buf[0:halo, :] = jnp.zeros((halo, gw), F32)

    @pl.when(c > 0)
    def _():
        for buf in (qbuf, kbuf, vbuf):
            buf[0:halo, :] = buf[c_len:c_len + halo, :]

    def conv_silu(buf, x_ref, cw_ref):
        x = x_ref[...]
        buf[halo:halo + c_len, :] = x
        rows = buf[...]
        acc = cw_ref[DN_CONV - 1:DN_CONV, :] * x
        for j in range(DN_CONV - 1):
            shifted = pltpu.roll(rows, DN_CONV - 1 - j, 0)[halo:halo + c_len, :]
            acc = acc + cw_ref[j:j + 1, :] * shifted
        return _silu(acc)

    q_all = conv_silu(qbuf, q_ref, cwq_ref)
    k_all = conv_silu(kbuf, k_ref, cwk_ref)
    v_all = conv_silu(vbuf, v_ref, cwv_ref)
    gate_all = nw_ref[...] * _silu(z_ref[...])

    ba = ba_ref[...]
    nh = pl.num_programs(1) * n_grp
    lane = _iota2((c_len, LANES), 1)
    row64 = _iota2((c_len, c_len), 0)
    col64 = _iota2((c_len, c_len), 1)
    tri_incl = (row64 >= col64).astype(BF16)
    causal = row64 >= col64
    strict = row64 > col64
    g_all = -jnp.exp(alog_ref[...]) * _softplus(ba + dtb_ref[...])
    gc_all = _dot_split(g_all, tri_incl, 3, ones_left=True)

    grp = list(range(n_grp))
    sls = [slice(gi * d, (gi + 1) * d) for gi in grp]

    def l2n(x):
        return x * lax.rsqrt(jnp.sum(x * x, axis=1, keepdims=True) + 1e-6)

    def pick(mat, lane_idx):
        return jnp.sum(jnp.where(lane == lane_idx, mat, 0.0), axis=1, keepdims=True)

    def decay_of(gc):
        gc_col = jnp.broadcast_to(gc, (c_len, c_len))
        gc_row = jnp.sum(jnp.where(row64 == col64, gc_col, 0.0), axis=0, keepdims=True)
        return jnp.exp(jnp.where(causal, gc_col - gc_row, NEG_BIG))

    q = [l2n(q_all[:, sl]) * (d ** -0.5) for sl in sls]
    k = [l2n(k_all[:, sl]) for sl in sls]
    v = [v_all[:, sl] for sl in sls]
    beta = [_sigmoid(pick(ba, hg * n_grp + gi)) for gi in grp]
    gc = [pick(gc_all, nh + hg * n_grp + gi) for gi in grp]
    gc_last = [x[c_len - 1:c_len, :] for x in gc]
    decay = _each(decay_of, gc)
    eg = _each(jnp.exp, gc)
    kb = _each(lambda a, b: a * b, k, beta)
    gram = _each(lambda a, b, c: _dot_nt(jnp.concatenate([a, b], axis=0), c), kb, q, k)
    low = _each(lambda m, dc: jnp.where(strict, m[0:c_len, :] * dc, 0.0), gram, decay)
    qk = _each(lambda m, dc: m[c_len:2 * c_len, :] * dc, gram, decay)
    t_inv = _unit_lower_inverse(low, c_len, c_len)
    uw = _each(lambda t, a, b, c, e: _dot(t, jnp.concatenate([a * b, c * e], axis=1)), t_inv, v, beta, kb, eg)
    u = [m[:, 0:d] for m in uw]
    w = [m[:, d:2 * d] for m in uw]
    s = [s_ref[gi] for gi in grp]
    wq_s = _each(lambda a, b, e, st: _dot(jnp.concatenate([a, b * e], axis=0), st), w, q, eg, s)
    v_new = _each(lambda a, m: a - m[0:c_len, :], u, wq_s)
    qs = [m[c_len:2 * c_len, :] for m in wq_s]
    intra = _each(_dot, qk, v_new)
    kv = _each(lambda a, gl, g1, vn: _dot_tn(a * jnp.exp(gl - g1), vn), k, gc_last, gc, v_new)
    for gi in grp:
        s_ref[gi] = s[gi] * jnp.exp(gc_last[gi]) + kv[gi]
        out = qs[gi] + intra[gi]
        out = out * lax.rsqrt(jnp.mean(out * out, axis=1, keepdims=True) + DN_NORM_EPS)
        o_ref[:, sls[gi]] = (out * gate_all[:, sls[gi]]).astype(o_ref.dtype)


def _gated_deltanet(proj, tail, ba_blk, conv_w, alog_pad, dtb_pad, norm_w, bsz, seq, n_heads, n_grp):
    t_tot = bsz * seq
    nc = seq // CHUNK
    d = DN_HEAD_DIM
    assert n_heads % n_grp == 0
    n_hg = n_heads // n_grp
    gw = n_grp * d

    def col(group):
        return pl.BlockSpec((CHUNK, gw), lambda b, h, c, g=group: (b * nc + c, g * n_hg + h))

    def cw(group):
        return pl.BlockSpec((DN_CONV, gw), lambda b, h, c, g=group: (0, g * n_hg + h))

    vec = pl.BlockSpec((1, LANES), lambda b, h, c: (0, 0))
    return pl.pallas_call(
        _dn_kernel, grid=(bsz, n_hg, nc),
        in_specs=[col(0), col(1), col(2), col(3),
                  pl.BlockSpec((CHUNK, LANES), lambda b, h, c: (b * nc + c, ba_blk)),
                  cw(0), cw(1), cw(2), vec, vec,
                  pl.BlockSpec((1, gw), lambda b, h, c: (0, 0))],
        out_specs=pl.BlockSpec((CHUNK, gw), lambda b, h, c: (b * nc + c, h)),
        out_shape=jax.ShapeDtypeStruct((t_tot, n_heads * d), BF16),
        scratch_shapes=[pltpu.VMEM((CHUNK + SUBLANES, gw), F32)] * 3 + [pltpu.VMEM((n_grp, d, d), F32)],
        compiler_params=_cparams(("arbitrary", "arbitrary", "arbitrary")), name="gated_deltanet",
    )(proj, proj, proj, proj, tail, conv_w, conv_w, conv_w, alog_pad, dtb_pad, jnp.tile(norm_w, (1, n_grp)))


def _rw_kernel(has_vres, *refs):
    if has_vres:
        (r_ref, k_ref, v_ref, lo_ref, vf_ref, mur_ref, muk_ref, muv_ref, mulo_ref, w0_ref, a0_ref,
         kk_ref, ka_ref, rk_ref, gw_ref, gb_ref, w2_ref, a2_ref, g2_ref, v0_ref, v2_ref,
         o_ref, rbuf, kbuf, vbuf, lobuf, ht_ref) = refs
    else:
        (r_ref, k_ref, v_ref, lo_ref, mur_ref, muk_ref, muv_ref, mulo_ref, w0_ref, a0_ref,
         kk_ref, ka_ref, rk_ref, gw_ref, gb_ref, w2_ref, a2_ref, g2_ref,
         o_ref, vf_out_ref, rbuf, kbuf, vbuf, lobuf, ht_ref) = refs
    c_len, n = CHUNK, RW_HEAD_DIM
    n_grp = ht_ref.shape[0]
    c = pl.program_id(2)
    halo = SUBLANES
    bufs = (rbuf, kbuf, vbuf, lobuf)

    @pl.when(c == 0)
    def _():
        ht_ref[...] = jnp.zeros_like(ht_ref)
        for buf in bufs:
            buf[0:halo, :] = jnp.zeros((halo, buf.shape[1]), F32)

    @pl.when(c > 0)
    def _():
        for buf in bufs:
            buf[0:halo, :] = buf[c_len:c_len + halo, :]

    def shift_mix(buf, x_ref, mu_ref):
        x = x_ref[...]
        buf[halo:halo + c_len, :] = x
        prev = pltpu.roll(buf[...], 1, 0)[halo:halo + c_len, :]
        return x + (prev - x) * mu_ref[...]

    r_all = shift_mix(rbuf, r_ref, mur_ref)
    k_all = shift_mix(kbuf, k_ref, muk_ref)
    v_all = shift_mix(vbuf, v_ref, muv_ref)
    lo = shift_mix(lobuf, lo_ref, mulo_ref)
    w_pre = w0_ref[...] + _dot(jnp.tanh(lo[:, 0:128]), w2_ref[...])
    w_log = -_softplus(-w_pre) - 0.5
    logd_all = -jnp.exp(w_log)
    a_all = _sigmoid(a0_ref[...] + _dot(lo[:, 128:256], a2_ref[...]))
    g_all = _dot(_sigmoid(lo[:, 256:512]), g2_ref[...])
    if has_vres:
        v_all = v_all + (vf_ref[...] - v_all) * _sigmoid(v0_ref[...] + _dot(lo[:, 512:640], v2_ref[...]))
    else:
        vf_out_ref[...] = v_all
    kk_all = k_all * kk_ref[...]
    k_all = k_all * (1.0 + (a_all - 1.0) * ka_ref[...])
    rk_all = r_all * k_all * rk_ref[...]

    row64 = _iota2((c_len, c_len), 0)
    col64 = _iota2((c_len, c_len), 1)
    tri_incl = (row64 >= col64).astype(BF16)
    gcum_all = _dot_split(logd_all, tri_incl, 3, ones_left=True)
    p_incl_all = jnp.exp(gcum_all)
    p_prev_all = jnp.exp(gcum_all - logd_all)
    p_inv_all = jnp.exp(-gcum_all)

    r128 = _iota2((LANES, LANES), 0)
    c128 = _iota2((LANES, LANES), 1)
    same_head = (r128 >> 6) == (c128 >> 6)
    seg_ones = same_head.astype(BF16)
    strict = r128 > c128
    incl = r128 >= c128
    head0 = _iota2((c_len, LANES), 1) < n

    def seg_sum(x):
        return _dot_split(x, seg_ones, 2)

    def stack(x):
        return jnp.concatenate([jnp.where(head0, x, 0.0), jnp.where(head0, 0.0, x)], axis=0)

    grp = list(range(n_grp))
    sls = [slice(gi * LANES, (gi + 1) * LANES) for gi in grp]

    def cut(x):
        return [x[:, sl] for sl in sls]

    r, k, v, a = cut(r_all), cut(k_all), cut(v_all), cut(a_all)
    p_incl, p_prev, p_inv = cut(p_incl_all), cut(p_prev_all), cut(p_inv_all)
    kk = cut(kk_all)
    kk_ss = _each(lambda x: seg_sum(x * x), kk)
    kk = _each(lambda x, ss: x * lax.rsqrt(ss + 1e-12), kk, kk_ss)
    a_s = _each(lambda x, p: stack(-x * p), kk, p_prev)
    r_s = _each(lambda x, p: stack(x * p), r, p_incl)
    b_s = _each(lambda x, y, p: stack(x * y * p), kk, a, p_inv)
    k_s = _each(lambda x, p: stack(x * p), k, p_inv)
    v_s = _each(stack, v)

    ar_s = _each(lambda x, y: jnp.concatenate([x, y], axis=0), a_s, r_s)
    m_b = _each(_dot_nt, ar_s, b_s)
    m_k = _each(_dot_nt, ar_s, k_s)
    a_ab = [m[0:2 * c_len, :] for m in m_b]
    a_rb = [m[2 * c_len:4 * c_len, :] for m in m_b]
    a_ak = [m[0:2 * c_len, :] for m in m_k]
    a_rk = [m[2 * c_len:4 * c_len, :] for m in m_k]
    t_inv = _unit_lower_inverse(_each(lambda m: jnp.where(strict, -m, 0.0), a_ab), 2 * c_len, c_len)
    akv = _each(lambda m, x: _dot(jnp.where(strict, m, 0.0), x), a_ak, v_s)
    tw = _each(lambda t, x, y: _dot(t, jnp.concatenate([x, y], axis=1)), t_inv, akv, a_s)
    u0 = [m[:, 0:LANES] for m in tw]
    w = [m[:, LANES:2 * LANES] for m in tw]

    ht = [ht_ref[gi] for gi in grp]
    wr_h = _each(lambda x, y, h: _dot_nt(jnp.concatenate([x, y], axis=0), h), w, r_s, ht)
    u = _each(lambda m, x: m[0:2 * c_len, :] + x, wr_h, u0)
    y_h = [m[2 * c_len:4 * c_len, :] for m in wr_h]
    uv = _each(lambda x, y: jnp.concatenate([x, y], axis=0), u, v_s)
    y_uv = _each(lambda mb, mk, x: _dot(jnp.concatenate([jnp.where(incl, mb, 0.0), jnp.where(incl, mk, 0.0)],
                                                        axis=1), x), a_rb, a_rk, uv)
    h_uv = _each(lambda x, b, k: _dot_tn(x, jnp.concatenate([b, k], axis=0)), uv, b_s, k_s)
    y = []
    for gi in grp:
        p_last = p_incl[gi][c_len - 1:c_len, :]
        ht_ref[gi] = jnp.where(same_head, (ht[gi] + h_uv[gi]) * p_last, 0.0)
        y_s = y_h[gi] + y_uv[gi]
        y.append(y_s[0:c_len, :] + y_s[c_len:2 * c_len, :])

    mu = _each(lambda x: seg_sum(x) * (1.0 / n), y)
    yc = _each(lambda x, m: x - m, y, mu)
    var = _each(lambda x: seg_sum(x * x) * (1.0 / n), yc)
    bonus = _each(seg_sum, cut(rk_all))
    for gi in grp:
        sl = sls[gi]
        yn = yc[gi] * lax.rsqrt(var[gi] + RW_GN_EPS) * gw_ref[:, sl] + gb_ref[:, sl]
        o_ref[:, sl] = ((yn + bonus[gi] * v[gi]) * g_all[:, sl]).astype(o_ref.dtype)


def _rwkv7(proj, tail, r_blk0, v_first, prm, bsz, seq, n_pairs, n_grp):
    t_tot = bsz * seq
    nc = seq // CHUNK
    w_rw = n_pairs * LANES
    gw = n_grp * LANES
    assert n_pairs % n_grp == 0 and r_blk0 % n_grp == 0
    n_pg = n_pairs // n_grp
    has_vres = v_first is not None

    def col(group):
        return pl.BlockSpec((CHUNK, gw), lambda b, p, c, g=group: (b * nc + c, r_blk0 // n_grp + g * n_pg + p))

    tok = pl.BlockSpec((CHUNK, gw), lambda b, p, c: (b * nc + c, p))
    lo_spec = pl.BlockSpec((CHUNK, 5 * LANES), lambda b, p, c: (b * nc + c, 0))
    vec = pl.BlockSpec((1, gw), lambda b, p, c: (0, p))
    vec_lo = pl.BlockSpec((1, 5 * LANES), lambda b, p, c: (0, 0))

    def lowrank(rank):
        return pl.BlockSpec((rank, gw), lambda b, p, c: (0, p))

    in_specs = [col(0), col(1), col(2), lo_spec]
    args = [proj, proj, proj, tail]
    if has_vres:
        in_specs.append(tok)
        args.append(v_first)
    in_specs += [vec, vec, vec, vec_lo] + [vec] * 7 + [lowrank(128), lowrank(128), lowrank(256)]
    args += [prm["mu_r"], prm["mu_k"], prm["mu_v"], prm["mu_lo"], prm["w0"], prm["a0"], prm["k_k"],
             prm["k_a"], prm["r_k"], prm["gn_w"], prm["gn_b"], prm["w2"], prm["a2"], prm["g2"]]
    if has_vres:
        in_specs += [vec, lowrank(128)]
        args += [prm["v0"], prm["v2"]]
    out_shape = [jax.ShapeDtypeStruct((t_tot, w_rw), BF16)]
    out_specs = [tok]
    if not has_vres:
        out_shape.append(jax.ShapeDtypeStruct((t_tot, w_rw), F32))
        out_specs.append(tok)
    halo_rows = CHUNK + SUBLANES
    outs = pl.pallas_call(
        functools.partial(_rw_kernel, has_vres), grid=(bsz, n_pg, nc),
        in_specs=in_specs, out_specs=out_specs, out_shape=out_shape,
        scratch_shapes=[pltpu.VMEM((halo_rows, gw), F32)] * 3
        + [pltpu.VMEM((halo_rows, 5 * LANES), F32), pltpu.VMEM((n_grp, LANES, LANES), F32)],
        compiler_params=_cparams(("arbitrary", "arbitrary", "arbitrary")), name="rwkv7",
    )(*args)
    if has_vres:
        return outs[0], v_first
    return outs[0], outs[1]


def _layer_norm_rows(y, g, b):
    mu = jnp.mean(y, axis=1, keepdims=True)
    yc = y - mu
    var = jnp.mean(yc * yc, axis=1, keepdims=True)
    return yc * lax.rsqrt(var + LN_EPS) * g + b


def _pack_bf16_halves(x):
    half = x.shape[1] // 2
    xr = x.astype(BF16).astype(F32)
    lo = lax.shift_right_logical(lax.bitcast_convert_type(xr[:, :half], jnp.uint32), jnp.uint32(16))
    hi = lax.bitcast_convert_type(xr[:, half:], jnp.uint32) & jnp.uint32(0xFFFF0000)
    return hi | lo


def _unpack_bf16_halves(w):
    lo = lax.bitcast_convert_type(lax.shift_left(w, jnp.uint32(16)), F32)
    hi = lax.bitcast_convert_type(w & jnp.uint32(0xFFFF0000), F32)
    return jnp.concatenate([lo, hi], axis=1).astype(BF16)


def _ln_router_kernel(y_ref, g_ref, b_ref, rw_ref, rb_ref, x_ref, xp_ref, ti_ref, tw_ref, pos_ref, cnt_ref,
                      base_ref):
    tm = y_ref.shape[0]
    i = pl.program_id(0)

    @pl.when(i == 0)
    def _():
        base_ref[...] = jnp.zeros_like(base_ref)

    x = _layer_norm_rows(y_ref[...], g_ref[...], b_ref[...])
    x_ref[...] = x
    xp_ref[...] = _pack_bf16_halves(x)
    logits = _dot(x, rw_ref[...]) + rb_ref[...]
    lane = _iota2((tm, LANES), 1)
    lane_f = lane.astype(F32)
    work = logits
    sels, vals, idxs = [], [], []
    for _ in range(TOP_K):
        m = jnp.max(work, axis=1, keepdims=True)
        idx = jnp.min(jnp.where(work == m, lane_f, float(LANES)), axis=1, keepdims=True)
        sel = lane_f == idx
        sels.append(sel)
        vals.append(m)
        idxs.append(idx)
        work = jnp.where(sel, 2.0 * NEG_BIG, work)
    exps = [jnp.exp(m - vals[0]) for m in vals]
    denom = exps[0] + exps[1] + exps[2] + exps[3]
    cnt = jnp.zeros((tm, LANES), F32)
    for sel in sels:
        cnt = cnt + sel.astype(F32)
    rowt = _iota2((tm, tm), 0)
    colt = _iota2((tm, tm), 1)
    before = _dot((rowt > colt).astype(F32), cnt)
    posmat = base_ref[...] + before
    ti = jnp.zeros((tm, LANES), F32)
    tw = jnp.zeros((tm, LANES), F32)
    pos = jnp.zeros((tm, LANES), F32)
    for kk in range(TOP_K):
        here = lane == kk
        ti = jnp.where(here, idxs[kk], ti)
        tw = jnp.where(here, exps[kk] / denom, tw)
        pos = jnp.where(here, jnp.sum(jnp.where(sels[kk], posmat, 0.0), axis=1, keepdims=True), pos)
    ti_ref[...] = ti.astype(I32)
    tw_ref[...] = tw
    pos_ref[...] = pos.astype(I32)
    base_new = base_ref[...] + jnp.sum(cnt, axis=0, keepdims=True)
    base_ref[...] = base_new
    cnt_ref[...] = jnp.broadcast_to(base_new, cnt_ref.shape).astype(I32)


def _ln_router(y, g, b, rw_pad, rb_pad, tm):
    t_tot, dm = y.shape
    tm = min(tm, t_tot)
    assert t_tot % tm == 0
    row = pl.BlockSpec((tm, dm), lambda i: (i, 0))
    vec = pl.BlockSpec((1, dm), lambda i: (0, 0))
    small = pl.BlockSpec((tm, LANES), lambda i: (i, 0))
    return pl.pallas_call(
        _ln_router_kernel, grid=(t_tot // tm,),
        in_specs=[row, vec, vec, pl.BlockSpec((dm, LANES), lambda i: (0, 0)),
                  pl.BlockSpec((1, LANES), lambda i: (0, 0))],
        out_specs=[row, pl.BlockSpec((tm, dm // 2), lambda i: (i, 0)), small, small, small,
                   pl.BlockSpec((SUBLANES, LANES), lambda i: (0, 0))],
        out_shape=[jax.ShapeDtypeStruct((t_tot, dm), F32),
                   jax.ShapeDtypeStruct((t_tot, dm // 2), jnp.uint32),
                   jax.ShapeDtypeStruct((t_tot, LANES), I32),
                   jax.ShapeDtypeStruct((t_tot, LANES), F32),
                   jax.ShapeDtypeStruct((t_tot, LANES), I32),
                   jax.ShapeDtypeStruct((SUBLANES, LANES), I32)],
        scratch_shapes=[pltpu.VMEM((1, LANES), F32)],
        compiler_params=_cparams(("arbitrary",)), name="ln_router",
    )(y, g, b, rw_pad, rb_pad)


def _row_gather_start(src3, dst3, sem, idx_of_row, n_rows):
    def body(g, carry):
        for j in range(SUBLANES):
            idx = idx_of_row(g * SUBLANES + j)
            pltpu.make_async_copy(src3.at[idx >> 3, pl.ds(idx & (SUBLANES - 1), 1)],
                                  dst3.at[g, pl.ds(j, 1)], sem).start()
        return carry
    assert n_rows % SUBLANES == 0
    lax.fori_loop(0, n_rows // SUBLANES, body, 0)


def _row_gather_wait(dst3, sem):
    pltpu.make_async_copy(dst3, dst3, sem).wait()


def _moe_up_kernel(te_ref, nu_ref, tos_ref, x_hbm, wg_ref, wu_ref, bg_ref, bu_ref, h_ref, xbuf, sem):
    tm = h_ref.shape[0]
    i = pl.program_id(0)
    slot = lax.rem(i, 2)
    n_used = nu_ref[0]

    def start(tile, s):
        _row_gather_start(x_hbm, xbuf.at[s], sem.at[s], lambda r: tos_ref[tile * tm + r], tm)

    @pl.when(i == 0)
    def _():
        start(0, 0)

    @pl.when(i + 1 < n_used)
    def _():
        start(i + 1, 1 - slot)

    @pl.when(i < n_used)
    def _():
        _row_gather_wait(xbuf.at[slot], sem.at[slot])
        x = _unpack_bf16_halves(xbuf[slot].reshape(tm, xbuf.shape[-1]))
        gate = jnp.dot(x, wg_ref[...], preferred_element_type=F32) + bg_ref[...]
        up = jnp.dot(x, wu_ref[...], preferred_element_type=F32) + bu_ref[...]
        gate = jnp.minimum(gate, SWIGLU_LIMIT)
        up = jnp.clip(up, -SWIGLU_LIMIT, SWIGLU_LIMIT)
        h_ref[...] = ((up + 1.0) * gate * _sigmoid(gate * SWIGLU_ALPHA)).astype(h_ref.dtype)

    @pl.when(i >= n_used)
    def _():
        h_ref[...] = jnp.zeros_like(h_ref)


def _moe_down_kernel(te_ref, nu_ref, h_ref, wd_ref, bd_ref, y_ref, wbf):
    i = pl.program_id(0)
    new_expert = jnp.logical_or(i == 0, te_ref[i] != te_ref[jnp.maximum(i - 1, 0)])

    @pl.when(jnp.logical_and(new_expert, i < nu_ref[0]))
    def _():
        wbf[...] = wd_ref[...].astype(BF16)

    @pl.when(i < nu_ref[0])
    def _():
        y_ref[...] = (jnp.dot(h_ref[...], wbf[...], preferred_element_type=F32) + bd_ref[...]).astype(y_ref.dtype)

    @pl.when(i >= nu_ref[0])
    def _():
        y_ref[...] = jnp.zeros_like(y_ref)


def _moe_experts(xp, tile_expert, n_used, tok_of_slot, wg, wu, bg, bu, wd_all, layer, bd, tm):
    n_slots = tok_of_slot.shape[0]
    n_tiles = n_slots // tm
    _, n_exp, dm, de = wg.shape
    h = pl.pallas_call(
        _moe_up_kernel,
        grid_spec=pltpu.PrefetchScalarGridSpec(
            num_scalar_prefetch=3, grid=(n_tiles,),
            in_specs=[pl.BlockSpec(memory_space=pl.ANY),
                      pl.BlockSpec((None, None, dm, de), lambda i, te, nu, tos: (layer, te[i], 0, 0)),
                      pl.BlockSpec((None, None, dm, de), lambda i, te, nu, tos: (layer, te[i], 0, 0)),
                      pl.BlockSpec((None, 1, de), lambda i, te, nu, tos: (te[i], 0, 0)),
                      pl.BlockSpec((None, 1, de), lambda i, te, nu, tos: (te[i], 0, 0))],
            out_specs=pl.BlockSpec((tm, de), lambda i, te, nu, tos: (i, 0)),
            scratch_shapes=[pltpu.VMEM((2, tm // SUBLANES, SUBLANES, dm // 2), jnp.uint32),
                            pltpu.SemaphoreType.DMA((2,))]),
        out_shape=jax.ShapeDtypeStruct((n_slots, de), BF16),
        compiler_params=_cparams(("arbitrary",)), name="moe_gate_up",
    )(tile_expert, n_used, tok_of_slot, xp.reshape(-1, SUBLANES, dm // 2), wg, wu, bg, bu)
    return pl.pallas_call(
        _moe_down_kernel,
        grid_spec=pltpu.PrefetchScalarGridSpec(
            num_scalar_prefetch=2, grid=(n_tiles,),
            in_specs=[pl.BlockSpec((tm, de), lambda i, te, nu: (i, 0)),
                      pl.BlockSpec((None, None, de, dm), lambda i, te, nu: (layer, te[i], 0, 0)),
                      pl.BlockSpec((None, 1, dm), lambda i, te, nu: (te[i], 0, 0))],
            out_specs=pl.BlockSpec((tm, dm), lambda i, te, nu: (i, 0)),
            scratch_shapes=[pltpu.VMEM((de, dm), BF16)]),
        out_shape=jax.ShapeDtypeStruct((n_slots, dm), F32),
        compiler_params=_cparams(("arbitrary",)), name="moe_down",
    )(tile_expert, n_used, h, wd_all, bd)


def _combine_ln_kernel(alpha, dest_ref, ys_hbm, x_ref, tw_ref, g_ref, b_ref, o_ref, ob_ref, ybuf, sem):
    tm = x_ref.shape[0]
    i = pl.program_id(0)
    n = pl.num_programs(0)
    slot = lax.rem(i, 2)

    def start(tile, s):
        for kk in range(TOP_K):
            _row_gather_start(ys_hbm, ybuf.at[s, kk], sem.at[s],
                              lambda r, kk=kk: dest_ref[(tile * tm + r) * TOP_K + kk], tm)

    @pl.when(i == 0)
    def _():
        start(0, 0)

    @pl.when(i + 1 < n)
    def _():
        start(i + 1, 1 - slot)

    for kk in range(TOP_K):
        _row_gather_wait(ybuf.at[slot, kk], sem.at[slot])
    tw = tw_ref[...]
    acc = alpha * x_ref[...]
    for kk in range(TOP_K):
        acc = acc + tw[:, kk:kk + 1] * ybuf[slot, kk].reshape(tm, ybuf.shape[-1])
    out = _layer_norm_rows(acc, g_ref[...], b_ref[...])
    o_ref[...] = out
    ob_ref[...] = out.astype(BF16)


def _combine_ln(ys, x, tw, dest_flat, g, b, alpha, tm):
    t_tot, dm = x.shape
    tm = min(tm, t_tot)
    assert t_tot % tm == 0
    row = pl.BlockSpec((tm, dm), lambda i, d: (i, 0))
    vec = pl.BlockSpec((1, dm), lambda i, d: (0, 0))
    return pl.pallas_call(
        functools.partial(_combine_ln_kernel, alpha),
        grid_spec=pltpu.PrefetchScalarGridSpec(
            num_scalar_prefetch=1, grid=(t_tot // tm,),
            in_specs=[pl.BlockSpec(memory_space=pl.ANY), row,
                      pl.BlockSpec((tm, LANES), lambda i, d: (i, 0)), vec, vec],
            out_specs=[row, row],
            scratch_shapes=[pltpu.VMEM((2, TOP_K, tm // SUBLANES, SUBLANES, dm), F32),
                            pltpu.SemaphoreType.DMA((2,))]),
        out_shape=[jax.ShapeDtypeStruct((t_tot, dm), F32), jax.ShapeDtypeStruct((t_tot, dm), BF16)],
        compiler_params=_cparams(("arbitrary",)), name="moe_combine_ln",
    )(dest_flat, ys.reshape(-1, SUBLANES, dm), x, tw, g, b)


def _tiles(t_tot):
    return dict(mm_m=min(1024, t_tot), mm_n=1024, mmw_n=512, ln_m=min(256, t_tot), moe_m=min(512, t_tot),
                comb_m=min(128, t_tot), rec_grp=16)


def _pad_cols(w, width):
    return jnp.pad(w, ((0, 0), (0, width - w.shape[1])))


def _pad_rows(w, rows):
    return jnp.pad(w, ((0, rows - w.shape[0]), (0, 0)))


def kernel(x, mix_w_in, mix_w_in_vres, dn_conv_w, dn_a_log, dn_dt_bias, dn_norm_w, rw_shift_mu, rw_shift_mu_vres, rw_w0, rw_w2, rw_a0, rw_a2, rw_g2, rw_v0, rw_v2, rw_k_k, rw_k_a, rw_r_k, rw_gn_w, rw_gn_b, mix_w_out, ln1_g, ln1_b, router_w, router_b, exp_w_gate, exp_b_gate, exp_w_up, exp_b_up, exp_w_down, exp_b_down, ln2_g, ln2_b):
    bsz, seq, dm = x.shape
    depth = mix_w_in.shape[0]
    t_tot = bsz * seq
    dn_heads = dn_a_log.shape[1]
    w_dn = dn_heads * DN_HEAD_DIM
    w_rw = rw_w0.shape[1]
    n_pairs = w_rw // LANES
    r_dec, r_aaa, r_gate, r_mv = rw_w2.shape[1], rw_a2.shape[1], rw_g2.shape[1], rw_v2.shape[1]
    n_exp = router_w.shape[2]
    assert seq % CHUNK == 0 and w_rw % LANES == 0 and 2 * dn_heads <= LANES
    assert max(r_dec, r_aaa, r_mv) <= LANES and r_gate == 2 * LANES and n_exp <= LANES
    alpha = (2 * depth) ** 0.25
    tl = _tiles(t_tot)
    n_dn_in = 4 * w_dn + 2 * dn_heads

    x2d = x.reshape(t_tot, dm)
    xb = x2d.astype(BF16)
    w_in_t = jnp.swapaxes(mix_w_in, 1, 2)
    w_vres_t = jnp.swapaxes(mix_w_in_vres, 1, 2)
    wg_bf, wu_bf = exp_w_gate.astype(BF16), exp_w_up.astype(BF16)
    v_first = None
    for l in range(depth):
        rw0 = n_dn_in
        o_lo = rw0 + 3 * w_rw
        o_a, o_g = o_lo + r_dec, o_lo + r_dec + r_aaa
        v_lo_t = w_vres_t[l - 1] if l > 0 else jnp.zeros((r_mv, dm), F32)
        w_tail_t = jnp.concatenate([
            _pad_rows(w_in_t[l, o_lo:o_a], LANES), _pad_rows(w_in_t[l, o_a:o_g], LANES), w_in_t[l, o_g:o_g + r_gate],
            _pad_rows(v_lo_t, LANES), _pad_rows(w_in_t[l, 4 * w_dn:n_dn_in], LANES)], axis=0)
        proj_dn = _matmul_wcast([xb], w_in_t, l, 4 * w_dn, tl["mm_m"], tl["mm_n"], F32, w_is_t=True,
                                w_single_buffer=True, name="in_proj_dn")
        proj_rw = _matmul_wcast([xb], w_in_t[l, rw0:o_lo], 0, 3 * w_rw, tl["mm_m"], tl["mm_n"], F32, w_is_t=True,
                                w_single_buffer=True, name="in_proj_rw")
        tail = _matmul_wcast([xb], w_tail_t, 0, w_tail_t.shape[0], tl["mm_m"], w_tail_t.shape[0], F32, w_is_t=True,
                             name="in_proj_tail")

        zeros_h = jnp.zeros((dn_heads,), F32)
        alog_pad = _pad_cols(jnp.concatenate([zeros_h, dn_a_log[l]])[None, :], LANES)
        dtb_pad = _pad_cols(jnp.concatenate([zeros_h, dn_dt_bias[l]])[None, :], LANES)
        o_dn = _gated_deltanet(proj_dn, tail, 5, dn_conv_w[l], alog_pad, dtb_pad, dn_norm_w[l][None, :],
                               bsz, seq, dn_heads, min(tl["rec_grp"], dn_heads))

        mu = rw_shift_mu[l]
        mu_lo_parts = [_pad_cols(mu[None, 3 * w_rw:3 * w_rw + r_dec], LANES),
                       _pad_cols(mu[None, 3 * w_rw + r_dec:3 * w_rw + r_dec + r_aaa], LANES),
                       mu[None, 3 * w_rw + r_dec + r_aaa:],
                       _pad_cols(rw_shift_mu_vres[l - 1][None, :] if l > 0 else jnp.zeros((1, r_mv), F32), LANES)]
        prm = dict(mu_r=mu[None, :w_rw], mu_k=mu[None, w_rw:2 * w_rw], mu_v=mu[None, 2 * w_rw:3 * w_rw],
                   mu_lo=jnp.concatenate(mu_lo_parts, axis=1),
                   w0=rw_w0[l][None, :], a0=rw_a0[l][None, :], k_k=rw_k_k[l][None, :], k_a=rw_k_a[l][None, :],
                   r_k=rw_r_k[l].reshape(1, w_rw), gn_w=rw_gn_w[l][None, :], gn_b=rw_gn_b[l][None, :],
                   w2=_pad_rows(rw_w2[l], LANES), a2=_pad_rows(rw_a2[l], LANES), g2=rw_g2[l])
        if l > 0:
            prm["v0"] = rw_v0[l - 1][None, :]
            prm["v2"] = _pad_rows(rw_v2[l - 1], LANES)
        o_rw, v_first = _rwkv7(proj_rw, tail, 0, v_first if l > 0 else None, prm, bsz, seq, n_pairs,
                               min(tl["rec_grp"], n_pairs))

        y1 = _matmul_wcast([o_dn, o_rw], mix_w_out, l, dm, tl["mm_m"], tl["mmw_n"], F32, res=x2d, alpha=alpha,
                           name="out_proj")
        rw_pad = _pad_cols(router_w[l], LANES)
        rb_pad = jnp.concatenate([router_b[l], jnp.full((LANES - n_exp,), NEG_BIG, F32)])[None, :]
        x1, x1p, top_i, top_w, pos, cnt = _ln_router(y1, ln1_g[l][None, :], ln1_b[l][None, :], rw_pad, rb_pad,
                                                     tl["ln_m"])

        tm = tl["moe_m"]
        counts = cnt[0, :n_exp]
        padded = ((counts + tm - 1) // tm) * tm
        ends = jnp.cumsum(padded)
        offs = ends - padded
        n_tiles = (t_tot * TOP_K) // tm + n_exp
        dest = offs[top_i[:, :TOP_K]] + pos[:, :TOP_K]
        tok_ids = jnp.broadcast_to(jnp.arange(t_tot, dtype=I32)[:, None], (t_tot, TOP_K))
        tok_of_slot = jnp.zeros((n_tiles * tm,), I32).at[dest.reshape(-1)].set(tok_ids.reshape(-1))
        tile_start = jnp.arange(n_tiles, dtype=I32) * tm
        tile_expert = jnp.minimum(jnp.sum((ends[None, :] <= tile_start[:, None]).astype(I32), axis=1), n_exp - 1)
        n_used = (ends[-1] // tm).astype(I32)[None]

        ys = _moe_experts(x1p, tile_expert, n_used, tok_of_slot,
                          wg_bf, wu_bf,
                          exp_b_gate[l][:, None, :], exp_b_up[l][:, None, :],
                          exp_w_down, l, exp_b_down[l][:, None, :], tm)
        x2d, xb = _combine_ln(ys, x1, top_w, dest.reshape(-1), ln2_g[l][None, :], ln2_b[l][None, :],
                              alpha, tl["comb_m"])
    return x2d.reshape(bsz, seq, dm)
```
